```python
import jax, jax.numpy as jnp
from jax import lax
import numpy as np

D_MODEL = 1024
BATCH = 8
SEQ = 4096
DEPTH = 4

N_MIXERS = 2
EXPAND = 2
D_INNER = EXPAND * D_MODEL
A_CHUNK = 128
A_GROUPS = 8
A_GROUP_DIM = D_INNER // A_GROUPS
B_HEADDIM = 64
B_HEADS = D_INNER // B_HEADDIM
B_GROUPS = 8
B_HEADS_PER_GROUP = B_HEADS // B_GROUPS
B_STATE = 128
B_CONV = 4
B_CHUNK = 128
B_GN = B_GROUPS * B_STATE
B_CONV_DIM = D_INNER + 2 * B_GN
B_IN_DIM = D_INNER + B_CONV_DIM + B_HEADS
N_A_LAYERS = (DEPTH + 1) // N_MIXERS
N_B_LAYERS = DEPTH // N_MIXERS
NORM_EPS = 1e-6
LN_EPS = 1e-5

kernel_name = "hybrid_sgu_ssd_interleaved"


def rmsnorm(x, w):
    xf = x.astype(jnp.float32)
    y = xf * lax.rsqrt(jnp.mean(xf * xf, axis=-1, keepdims=True) + NORM_EPS)
    return (y * w.astype(jnp.float32)).astype(x.dtype)


def layernorm(x, w, b):
    xf = x.astype(jnp.float32)
    mu = jnp.mean(xf, axis=-1, keepdims=True)
    var = jnp.mean(jnp.square(xf - mu), axis=-1, keepdims=True)
    y = (xf - mu) * lax.rsqrt(var + LN_EPS)
    return (y * w.astype(jnp.float32) + b.astype(jnp.float32)).astype(x.dtype)


def spatial_gating_mixer(h, w_in, ln_w, ln_b, w_s, b_s, w_out):
    bsz, L, _ = h.shape
    z, u, v = jnp.split(h @ w_in, 3, axis=-1)
    v = layernorm(v, ln_w, ln_b)
    nc = L // A_CHUNK
    v = v.reshape(bsz, nc, A_CHUNK, A_GROUPS, A_GROUP_DIM)
    causal = jnp.tril(jnp.ones((A_CHUNK, A_CHUNK), dtype=bool))
    w_causal = jnp.where(causal[None], w_s, jnp.zeros((), w_s.dtype))
    mixed = jnp.einsum('gts,bcsgd->bctgd', w_causal, v) + b_s.T[None, None, :, :, None]
    mixed = mixed.reshape(bsz, L, D_INNER)
    return (u * mixed * jax.nn.silu(z)) @ w_out


def causal_depthwise_conv(x, w, b):
    y = lax.conv_general_dilated(
        x, w[:, None, :].astype(x.dtype), window_strides=(1,), padding=[(B_CONV - 1, 0)],
        dimension_numbers=('NWC', 'WIO', 'NWC'), feature_group_count=x.shape[-1])
    return y + b


def ssd_mixer(h, w_in, conv_w, conv_b, dt_bias, a_log, d_skip, norm_w, w_out):
    bsz, L, _ = h.shape
    G, J, P, N = B_GROUPS, B_HEADS_PER_GROUP, B_HEADDIM, B_STATE
    proj = h @ w_in
    z = proj[..., :D_INNER]
    xbc = proj[..., D_INNER:D_INNER + B_CONV_DIM]
    dt_raw = proj[..., D_INNER + B_CONV_DIM:]
    xbc = jax.nn.silu(causal_depthwise_conv(xbc, conv_w, conv_b))
    nc = L // B_CHUNK
    x = xbc[..., :D_INNER].reshape(bsz, nc, B_CHUNK, G, J, P)
    Bm = xbc[..., D_INNER:D_INNER + B_GN].reshape(bsz, nc, B_CHUNK, G, N)
    Cm = xbc[..., D_INNER + B_GN:].reshape(bsz, nc, B_CHUNK, G, N)
    dt = jax.nn.softplus(dt_raw.astype(jnp.float32) + dt_bias.astype(jnp.float32))
    dt = dt.reshape(bsz, nc, B_CHUNK, G, J)
    A = -jnp.exp(a_log.astype(jnp.float32)).reshape(G, J)
    dA_cs = jnp.cumsum(dt * A, axis=2)

    seg = dA_cs[:, :, :, None] - dA_cs[:, :, None, :]
    causal = jnp.tril(jnp.ones((B_CHUNK, B_CHUNK), dtype=bool))[:, :, None, None]
    decay = jnp.exp(jnp.where(causal, seg, -jnp.inf))
    cb = jnp.einsum('bclgn,bcsgn->bclsg', Cm, Bm)
    w_ls = (cb[..., None].astype(jnp.float32) * decay * dt[:, :, None]).astype(x.dtype)
    y_diag = jnp.einsum('bclsgj,bcsgjp->bclgjp', w_ls, x)

    decay_to_end = jnp.exp(dA_cs[:, :, -1:] - dA_cs)
    states = jnp.einsum('bclgn,bclgj,bclgjp->bcgjpn', Bm, (decay_to_end * dt).astype(x.dtype), x)
    chunk_decay = jnp.exp(dA_cs[:, :, -1])

    def step(carry, inp):
        st, dec = inp
        return carry * dec[..., None, None] + st, carry

    init = jnp.zeros((bsz, G, J, P, N), jnp.float32)
    _, prev = lax.scan(step, init, (jnp.moveaxis(states.astype(jnp.float32), 1, 0),
                                    jnp.moveaxis(chunk_decay, 1, 0)))
    prev = jnp.moveaxis(prev, 0, 1)
    y_off = jnp.einsum('bclgn,bcgjpn,bclgj->bclgjp', Cm.astype(jnp.float32), prev, jnp.exp(dA_cs))

    y = y_diag + y_off.astype(x.dtype) + x * d_skip.reshape(G, J)[..., None].astype(x.dtype)
    y = y.reshape(bsz, L, D_INNER)
    gated = (y * jax.nn.silu(z)).reshape(bsz, L, B_GROUPS, D_INNER // B_GROUPS)
    gated = rmsnorm(gated, norm_w.reshape(B_GROUPS, D_INNER // B_GROUPS)).reshape(bsz, L, D_INNER)
    return gated @ w_out


def setup_inputs(seed: int = 0) -> dict:
    key = jax.random.key(seed)
    ks = jax.random.split(key, 20)
    f32 = jnp.float32
    nA, nB = N_A_LAYERS, N_B_LAYERS
    x = jax.random.normal(ks[0], (BATCH, SEQ, D_MODEL), f32)
    norm_w = 1.0 + 0.02 * jax.random.normal(ks[1], (DEPTH, D_MODEL), f32)
    final_norm_w = 1.0 + 0.02 * jax.random.normal(ks[2], (D_MODEL,), f32)
    a_w_in = jax.random.normal(ks[3], (nA, D_MODEL, 3 * D_INNER), f32) * D_MODEL ** -0.5
    a_ln_w = 1.0 + 0.02 * jax.random.normal(ks[4], (nA, D_INNER), f32)
    a_ln_b = 0.02 * jax.random.normal(ks[5], (nA, D_INNER), f32)
    a_w_s = jax.random.normal(ks[6], (nA, A_GROUPS, A_CHUNK, A_CHUNK), f32) * A_CHUNK ** -0.5
    a_b_s = 1.0 + 0.1 * jax.random.normal(ks[7], (nA, A_GROUPS, A_CHUNK), f32)
    a_w_out = jax.random.normal(ks[8], (nA, D_INNER, D_MODEL), f32) * D_INNER ** -0.5
    b_w_in = jax.random.normal(ks[9], (nB, D_MODEL, B_IN_DIM), f32) * D_MODEL ** -0.5
    b_conv_w = jax.random.normal(ks[10], (nB, B_CONV, B_CONV_DIM), f32) * B_CONV ** -0.5
    b_conv_b = 0.02 * jax.random.normal(ks[11], (nB, B_CONV_DIM), f32)
    dt0 = jnp.exp(jax.random.uniform(ks[12], (nB, B_HEADS), f32, np.log(1e-3), np.log(1e-1)))
    b_dt_bias = dt0 + jnp.log(-jnp.expm1(-dt0))
    b_a_log = jnp.log(jax.random.uniform(ks[13], (nB, B_HEADS), f32, 1.0, 16.0))
    b_d_skip = 1.0 + 0.1 * jax.random.normal(ks[14], (nB, B_HEADS), f32)
    b_norm_w = 1.0 + 0.02 * jax.random.normal(ks[15], (nB, D_INNER), f32)
    b_w_out = jax.random.normal(ks[16], (nB, D_INNER, D_MODEL), f32) * D_INNER ** -0.5
    return {"x": x, "norm_w": norm_w, "final_norm_w": final_norm_w,
            "a_w_in": a_w_in, "a_ln_w": a_ln_w, "a_ln_b": a_ln_b, "a_w_s": a_w_s, "a_b_s": a_b_s,
            "a_w_out": a_w_out,
            "b_w_in": b_w_in, "b_conv_w": b_conv_w, "b_conv_b": b_conv_b, "b_dt_bias": b_dt_bias,
            "b_a_log": b_a_log, "b_d_skip": b_d_skip, "b_norm_w": b_norm_w, "b_w_out": b_w_out}


def reference(x, norm_w, final_norm_w, a_w_in, a_ln_w, a_ln_b, a_w_s, a_b_s, a_w_out,
              b_w_in, b_conv_w, b_conv_b, b_dt_bias, b_a_log, b_d_skip, b_norm_w, b_w_out):
    h = x
    for i in range(DEPTH):
        hn = rmsnorm(h, norm_w[i])
        k = i // N_MIXERS
        if i % N_MIXERS == 0:
            out = spatial_gating_mixer(hn, a_w_in[k], a_ln_w[k], a_ln_b[k], a_w_s[k], a_b_s[k], a_w_out[k])
        else:
            out = ssd_mixer(hn, b_w_in[k], b_conv_w[k], b_conv_b[k], b_dt_bias[k], b_a_log[k],
                            b_d_skip[k], b_norm_w[k], b_w_out[k])
        h = h + out
    return rmsnorm(h, final_norm_w)
```

```python
import functools

import jax
import jax.numpy as jnp
from jax import lax
from jax.experimental import pallas as pl
from jax.experimental.pallas import tpu as pltpu

D_MODEL = 1024
D_INNER = 2048
CHUNK = 128
N_GROUPS = 8
GROUP_DIM = D_INNER // N_GROUPS
HEADDIM = 64
HEADS_PER_GROUP = GROUP_DIM // HEADDIM
N_HEADS = D_INNER // HEADDIM
D_STATE = 128
D_CONV = 4
BC_DIM = N_GROUPS * D_STATE
CONV_DIM = D_INNER + 2 * BC_DIM
NORM_EPS = 1e-6
LN_EPS = 1e-5
CARRY_ROWS = 8

TOKEN_TILE = 256
VMEM_LIMIT_BYTES = 56 * 1024 * 1024

BF16 = jnp.bfloat16
F32 = jnp.float32


def _dot(a, b):
    return jnp.dot(a, b, preferred_element_type=F32)


def _dot_nt(a, b):
    return lax.dot_general(a, b, (((1,), (1,)), ((), ())), preferred_element_type=F32)


def _dot_tn(a, b):
    return lax.dot_general(a, b, (((0,), (0,)), ((), ())), preferred_element_type=F32)


def _silu(x):
    return x * (1.0 / (1.0 + jnp.exp(-x)))


def _rms_rows(x, w):
    return x * lax.rsqrt(jnp.mean(x * x, axis=-1, keepdims=True) + NORM_EPS) * w


def _split3(a):
    hi = a.astype(BF16)
    r1 = a - hi.astype(F32)
    mid = r1.astype(BF16)
    lo = (r1 - mid.astype(F32)).astype(BF16)
    return hi, mid, lo


def _sgu_kernel(h_ref, nw_ref, win_ref, lnw_ref, lnb_ref, ws_ref, bst_ref, wout_ref,
                o_ref, v_scr, g_scr, *, tm):
    x = h_ref[0]
    hn = _rms_rows(x, nw_ref[...]).astype(BF16)

    v = _dot(hn, win_ref[:, 2 * D_INNER:3 * D_INNER])
    mu = jnp.mean(v, axis=-1, keepdims=True)
    vc = v - mu
    var = jnp.mean(vc * vc, axis=-1, keepdims=True)
    vn = vc * lax.rsqrt(var + LN_EPS) * lnw_ref[...] + lnb_ref[...]
    v_scr[...] = vn.astype(BF16)

    row = lax.broadcasted_iota(jnp.int32, (CHUNK, CHUNK), 0)
    col = lax.broadcasted_iota(jnp.int32, (CHUNK, CHUNK), 1)
    causal = row >= col
    for g in range(N_GROUPS):
        lo, hi = g * GROUP_DIM, (g + 1) * GROUP_DIM
        z = _dot(hn, win_ref[:, lo:hi])
        u = _dot(hn, win_ref[:, D_INNER + lo:D_INNER + hi])
        gate = u * _silu(z)
        wc = jnp.where(causal, ws_ref[g], 0.0).astype(BF16)
        bias = bst_ref[:, g:g + 1]
        for c in range(tm // CHUNK):
            r0, r1 = c * CHUNK, (c + 1) * CHUNK
            mixed = _dot(wc, v_scr[r0:r1, lo:hi]) + bias
            g_scr[r0:r1, lo:hi] = (gate[r0:r1] * mixed).astype(BF16)

    o_ref[0] = _dot(g_scr[...], wout_ref[...]) + x


def _sgu_layer(h, norm_w, w_in, ln_w, ln_b, w_s, b_s, w_out):
    bsz, seq, _ = h.shape
    tm = TOKEN_TILE
    const2 = lambda b, i: (0, 0)
    const3 = lambda b, i: (0, 0, 0)
    tile = lambda b, i: (b, i, 0)
    resident = pl.Buffered(1)
    return pl.pallas_call(
        functools.partial(_sgu_kernel, tm=tm),
        grid=(bsz, seq // tm),
        in_specs=[
            pl.BlockSpec((1, tm, D_MODEL), tile),
            pl.BlockSpec((1, D_MODEL), const2),
            pl.BlockSpec((D_MODEL, 3 * D_INNER), const2, pipeline_mode=resident),
            pl.BlockSpec((1, D_INNER), const2),
            pl.BlockSpec((1, D_INNER), const2),
            pl.BlockSpec((N_GROUPS, CHUNK, CHUNK), const3),
            pl.BlockSpec((CHUNK, N_GROUPS), const2),
            pl.BlockSpec((D_INNER, D_MODEL), const2, pipeline_mode=resident),
        ],
        out_specs=pl.BlockSpec((1, tm, D_MODEL), tile),
        out_shape=jax.ShapeDtypeStruct(h.shape, F32),
        scratch_shapes=[
            pltpu.VMEM((tm, D_INNER), BF16),
            pltpu.VMEM((tm, D_INNER), BF16),
        ],
        compiler_params=pltpu.CompilerParams(
            dimension_semantics=("arbitrary", "arbitrary"),
            vmem_limit_bytes=VMEM_LIMIT_BYTES),
        name="sgu_layer",
    )(h, norm_w.reshape(1, D_MODEL), w_in.astype(BF16), ln_w.reshape(1, D_INNER),
      ln_b.reshape(1, D_INNER), w_s, b_s.T, w_out.astype(BF16))


def _ssd_kernel(h_ref, nw_ref, wzx_ref, wdt_ref, wdtt_ref, cw_ref, cbias_ref, dtb_ref, dtbt_ref,
                alog_ref, alogt_ref, dskip_ref, gnw_ref, expand_ref, wout_ref, fnw_ref,
                o_ref, raw_scr, st_scr, y_scr, *, tm, final_norm):
    @pl.when(pl.program_id(1) == 0)
    def _():
        st_scr[...] = jnp.zeros_like(st_scr)
        raw_scr[0:CARRY_ROWS, :] = jnp.zeros((CARRY_ROWS, CONV_DIM), F32)

    x_in = h_ref[0]
    hn = _rms_rows(x_in, nw_ref[...]).astype(BF16)

    raw_scr[CARRY_ROWS:CARRY_ROWS + tm, :] = _dot(hn, wzx_ref[:, D_INNER:])

    dt = jax.nn.softplus(_dot(hn, wdt_ref[...]) + dtb_ref[...])
    dtt = jax.nn.softplus(_dot_nt(wdtt_ref[...], hn) + dtbt_ref[...])
    da = dt * (-jnp.exp(alog_ref[...]))
    dat = dtt * (-jnp.exp(alogt_ref[...]))

    row = lax.broadcasted_iota(jnp.int32, (CHUNK, CHUNK), 0)
    col = lax.broadcasted_iota(jnp.int32, (CHUNK, CHUNK), 1)
    causal = row >= col
    lower = causal.astype(BF16)
    upper = (row <= col).astype(BF16)

    n_chunks = tm // CHUNK
    cs, cst, dte, ecs, cdec = [], [], [], [], []
    for c in range(n_chunks):
        r0, r1 = c * CHUNK, (c + 1) * CHUNK
        cs_c = sum(_dot(lower, p) for p in _split3(da[r0:r1]))
        cst_c = sum(_dot(p, upper) for p in _split3(dat[:, r0:r1]))
        last = cs_c[CHUNK - 1:CHUNK, :]
        cs.append(cs_c)
        cst.append(cst_c)
        dte.append(jnp.exp(last - cs_c) * dt[r0:r1])
        ecs.append(jnp.exp(cs_c))
        cdec.append(jnp.broadcast_to(jnp.exp(last), (CARRY_ROWS, N_HEADS)))

    for g in range(N_GROUPS):
        lo, hi = g * GROUP_DIM, (g + 1) * GROUP_DIM
        blo, bhi = D_INNER + g * D_STATE, D_INNER + (g + 1) * D_STATE
        clo, chi = blo + BC_DIM, bhi + BC_DIM
        expand_g = expand_ref[:, lo:hi]

        def conv(c0, c1):
            acc = cbias_ref[:, c0:c1]
            for k in range(D_CONV):
                start = CARRY_ROWS - (D_CONV - 1) + k
                acc = acc + cw_ref[k:k + 1, c0:c1] * raw_scr[start:start + tm, c0:c1]
            return _silu(acc)

        xg = conv(lo, hi)
        bg = conv(blo, bhi).astype(BF16)
        cg = conv(clo, chi).astype(BF16)
        zg = _dot(hn, wzx_ref[:, lo:hi])
        xg_bf = xg.astype(BF16)

        for c in range(n_chunks):
            r0, r1 = c * CHUNK, (c + 1) * CHUNK
            b_c, c_c, x_c = bg[r0:r1], cg[r0:r1], xg_bf[r0:r1]
            cb = _dot_nt(c_c, b_c)
            y_heads = []
            for j in range(HEADS_PER_GROUP):
                hd = g * HEADS_PER_GROUP + j
                seg = cs[c][:, hd:hd + 1] - cst[c][hd:hd + 1, :]
                decay = jnp.exp(jnp.where(causal, seg, -jnp.inf))
                w_ls = (cb * decay * dtt[hd:hd + 1, r0:r1]).astype(BF16)
                y_heads.append(_dot(w_ls, x_c[:, j * HEADDIM:(j + 1) * HEADDIM]))
            y_diag = jnp.concatenate(y_heads, axis=1)

            dte_x = _dot(dte[c].astype(BF16), expand_g)
            ecs_x = _dot(ecs[c].astype(BF16), expand_g)
            cd_hi, cd_mid, _ = _split3(cdec[c])
            cd_x = (_dot(cd_hi, expand_g) + _dot(cd_mid, expand_g))[0:1, :]

            state = st_scr[g]
            y_off = _dot(c_c, state.astype(BF16)) * ecs_x
            xw = (xg[r0:r1] * dte_x).astype(BF16)
            st_scr[g] = state * cd_x + _dot_tn(b_c, xw)

            y = y_diag + y_off + xg[r0:r1] * dskip_ref[:, lo:hi]
            gated = y * _silu(zg[r0:r1])
            y_scr[r0:r1, lo:hi] = _rms_rows(gated, gnw_ref[:, lo:hi]).astype(BF16)

    raw_scr[0:CARRY_ROWS, :] = raw_scr[tm:tm + CARRY_ROWS, :]

    out = _dot(y_scr[...], wout_ref[...]) + x_in
    if final_norm:
        out = _rms_rows(out, fnw_ref[...])
    o_ref[0] = out


def _ssd_layer(h, norm_w, w_in, conv_w, conv_b, dt_bias, a_log, d_skip, gn_w, w_out, final_norm_w):
    bsz, seq, _ = h.shape
    tm = TOKEN_TILE
    const2 = lambda b, i: (0, 0)
    tile = lambda b, i: (b, i, 0)
    resident = pl.Buffered(1)
    w_zx = w_in[:, :D_INNER + CONV_DIM].astype(BF16)
    w_dt = w_in[:, D_INNER + CONV_DIM:].astype(BF16)
    expand = jnp.repeat(jnp.eye(N_HEADS, dtype=BF16), HEADDIM, axis=1)
    final = final_norm_w is not None
    fnw = final_norm_w if final else jnp.ones((D_MODEL,), F32)
    small = lambda shape: pl.BlockSpec(shape, const2)
    return pl.pallas_call(
        functools.partial(_ssd_kernel, tm=tm, final_norm=final),
        grid=(bsz, seq // tm),
        in_specs=[
            pl.BlockSpec((1, tm, D_MODEL), tile),
            small((1, D_MODEL)),
            pl.BlockSpec((D_MODEL, D_INNER + CONV_DIM), const2, pipeline_mode=resident),
            small((D_MODEL, N_HEADS)),
            small((N_HEADS, D_MODEL)),
            small((D_CONV, CONV_DIM)),
            small((1, CONV_DIM)),
            small((1, N_HEADS)),
            small((N_HEADS, 1)),
            small((1, N_HEADS)),
            small((N_HEADS, 1)),
            small((1, D_INNER)),
            small((1, D_INNER)),
            small((N_HEADS, D_INNER)),
            pl.BlockSpec((D_INNER, D_MODEL), const2, pipeline_mode=resident),
            small((1, D_MODEL)),
        ],
        out_specs=pl.BlockSpec((1, tm, D_MODEL), tile),
        out_shape=jax.ShapeDtypeStruct(h.shape, F32),
        scratch_shapes=[
            pltpu.VMEM((CARRY_ROWS + tm, CONV_DIM), F32),
            pltpu.VMEM((N_GROUPS, D_STATE, GROUP_DIM), F32),
            pltpu.VMEM((tm, D_INNER), BF16),
        ],
        compiler_params=pltpu.CompilerParams(
            dimension_semantics=("arbitrary", "arbitrary"),
            vmem_limit_bytes=VMEM_LIMIT_BYTES),
        name="ssd_final" if final else "ssd_layer",
    )(h, norm_w.reshape(1, D_MODEL), w_zx, w_dt, w_dt.T, conv_w, conv_b.reshape(1, CONV_DIM),
      dt_bias.reshape(1, N_HEADS), dt_bias.reshape(N_HEADS, 1),
      a_log.reshape(1, N_HEADS), a_log.reshape(N_HEADS, 1),
      jnp.repeat(d_skip, HEADDIM).reshape(1, D_INNER), gn_w.reshape(1, D_INNER),
      expand, w_out.astype(BF16), fnw.reshape(1, D_MODEL))


def kernel(x, norm_w, final_norm_w, a_w_in, a_ln_w, a_ln_b, a_w_s, a_b_s, a_w_out,
           b_w_in, b_conv_w, b_conv_b, b_dt_bias, b_a_log, b_d_skip, b_norm_w, b_w_out):
    depth = norm_w.shape[0]
    h = x
    for i in range(depth):
        k = i // 2
        if i % 2 == 0:
            h = _sgu_layer(h, norm_w[i], a_w_in[k], a_ln_w[k], a_ln_b[k], a_w_s[k], a_b_s[k],
                           a_w_out[k])
        else:
            h = _ssd_layer(h, norm_w[i], b_w_in[k], b_conv_w[k], b_conv_b[k], b_dt_bias[k],
                           b_a_log[k], b_d_skip[k], b_norm_w[k], b_w_out[k],
                           final_norm_w if i == depth - 1 else None)
    return h
```

```python
import functools

import jax
import jax.numpy as jnp
from jax import lax
from jax.experimental import pallas as pl
from jax.experimental.pallas import tpu as pltpu

D_MODEL = 1024
D_INNER = 2048
CHUNK = 128
N_GROUPS = 8
GROUP_DIM = D_INNER // N_GROUPS
HEADDIM = 64
HEADS_PER_GROUP = GROUP_DIM // HEADDIM
N_HEADS = D_INNER // HEADDIM
D_STATE = 128
D_CONV = 4
BC_DIM = N_GROUPS * D_STATE
CONV_DIM = D_INNER + 2 * BC_DIM
NORM_EPS = 1e-6
LN_EPS = 1e-5
CARRY_ROWS = 8

N_PAIRS = N_GROUPS // 2
PAIR_HEADS = 2 * HEADS_PER_GROUP
PAIR_DIM = 2 * GROUP_DIM
GROUP_RAW = GROUP_DIM + 2 * D_STATE
PAIR_COLS = PAIR_DIM + 2 * GROUP_RAW

TOKEN_TILE = 256
VMEM_LIMIT_BYTES = 60 * 1024 * 1024

BF16 = jnp.bfloat16
F32 = jnp.float32


def _dot(a, b):
    return jnp.dot(a, b, preferred_element_type=F32)


def _dot_nt(a, b):
    return lax.dot_general(a, b, (((1,), (1,)), ((), ())), preferred_element_type=F32)


def _dot_tn(a, b):
    return lax.dot_general(a, b, (((0,), (0,)), ((), ())), preferred_element_type=F32)


def _silu(x):
    return x * (1.0 / (1.0 + jnp.exp(-x)))


def _rms_rows(x, w):
    return x * lax.rsqrt(jnp.mean(x * x, axis=-1, keepdims=True) + NORM_EPS) * w


def _split3(a):
    hi = a.astype(BF16)
    r1 = a - hi.astype(F32)
    mid = r1.astype(BF16)
    lo = (r1 - mid.astype(F32)).astype(BF16)
    return hi, mid, lo


def _sgu_kernel(h_ref, nw_ref, win_ref, lnw_ref, lnb_ref, ws_ref, bst_ref, wout_ref,
                o_ref, v_scr, g_scr, *, tm):
    x = h_ref[0]
    hn = _rms_rows(x, nw_ref[...]).astype(BF16)

    v = _dot(hn, win_ref[:, 2 * D_INNER:3 * D_INNER])
    mu = jnp.mean(v, axis=-1, keepdims=True)
    vc = v - mu
    var = jnp.mean(vc * vc, axis=-1, keepdims=True)
    vn = vc * lax.rsqrt(var + LN_EPS) * lnw_ref[...] + lnb_ref[...]
    v_scr[...] = vn.astype(BF16)

    row = lax.broadcasted_iota(jnp.int32, (CHUNK, CHUNK), 0)
    col = lax.broadcasted_iota(jnp.int32, (CHUNK, CHUNK), 1)
    causal = row >= col
    for g in range(N_GROUPS):
        lo, hi = g * GROUP_DIM, (g + 1) * GROUP_DIM
        z = _dot(hn, win_ref[:, lo:hi])
        u = _dot(hn, win_ref[:, D_INNER + lo:D_INNER + hi])
        gate = u * _silu(z)
        wc = jnp.where(causal, ws_ref[g], 0.0).astype(BF16)
        bias = bst_ref[:, g:g + 1]
        for c in range(tm // CHUNK):
            r0, r1 = c * CHUNK, (c + 1) * CHUNK
            mixed = _dot(wc, v_scr[r0:r1, lo:hi]) + bias
            g_scr[r0:r1, lo:hi] = (gate[r0:r1] * mixed).astype(BF16)

    o_ref[0] = _dot(g_scr[...], wout_ref[...]) + x


def _sgu_layer(h, norm_w, w_in, ln_w, ln_b, w_s, b_s, w_out):
    bsz, seq, _ = h.shape
    tm = TOKEN_TILE
    const2 = lambda b, i: (0, 0)
    const3 = lambda b, i: (0, 0, 0)
    tile = lambda b, i: (b, i, 0)
    resident = pl.Buffered(1)
    return pl.pallas_call(
        functools.partial(_sgu_kernel, tm=tm),
        grid=(bsz, seq // tm),
        in_specs=[
            pl.BlockSpec((1, tm, D_MODEL), tile),
            pl.BlockSpec((1, D_MODEL), const2),
            pl.BlockSpec((D_MODEL, 3 * D_INNER), const2, pipeline_mode=resident),
            pl.BlockSpec((1, D_INNER), const2),
            pl.BlockSpec((1, D_INNER), const2),
            pl.BlockSpec((N_GROUPS, CHUNK, CHUNK), const3),
            pl.BlockSpec((CHUNK, N_GROUPS), const2),
            pl.BlockSpec((D_INNER, D_MODEL), const2, pipeline_mode=resident),
        ],
        out_specs=pl.BlockSpec((1, tm, D_MODEL), tile),
        out_shape=jax.ShapeDtypeStruct(h.shape, F32),
        scratch_shapes=[
            pltpu.VMEM((tm, D_INNER), BF16),
            pltpu.VMEM((tm, D_INNER), BF16),
        ],
        compiler_params=pltpu.CompilerParams(
            dimension_semantics=("arbitrary", "arbitrary"),
            vmem_limit_bytes=VMEM_LIMIT_BYTES),
        name="sgu_layer",
    )(h, norm_w.reshape(1, D_MODEL), w_in.astype(BF16), ln_w.reshape(1, D_INNER),
      ln_b.reshape(1, D_INNER), w_s, b_s.T, w_out.astype(BF16))


def _ssd_kernel(h_ref, hres_ref, nw_ref, wpair_ref, wdtt_ref, cw_ref, cbias_ref, dtb_ref, alog_ref,
                dskip_ref, gnw_ref, expand_ref, wout_ref, fnw_ref,
                o_ref, hn0, hn1, raw0, raw1, z0, z1, dtt0, dtt1, cst0, cst1, col0, col1, y0, y1,
                st_scr, acc_scr, *, tm, pairs_per_seq, final_norm):
    step = pl.program_id(0)

    @pl.when(step == 0)
    def _():
        for ref in (raw0, raw1, z0, z1, dtt0, dtt1, cst0, cst1, col0, col1, y0, y1, st_scr):
            ref[...] = jnp.zeros(ref.shape, ref.dtype)

    n_chunks = tm // CHUNK
    half = PAIR_DIM // 2

    def pair_body(q, carry, *, hn_a, raw_a, z_a, dtt_a, cst_a, col_a,
                  raw_b, z_b, dtt_b, cst_b, col_b, y_b, y_c, seq_start):
        heads = pl.ds(pl.multiple_of(q * PAIR_HEADS, PAIR_HEADS), PAIR_HEADS)
        hn = hn_a[...]
        row = lax.broadcasted_iota(jnp.int32, (CHUNK, CHUNK), 0)
        col = lax.broadcasted_iota(jnp.int32, (CHUNK, CHUNK), 1)
        causal = row >= col
        upper = (row <= col).astype(BF16)

        def projection_slice(i):
            if i < 2:
                z_a[q, :, i * half:(i + 1) * half] = _dot(hn, wpair_ref[q, :, i * half:(i + 1) * half])
            elif i < 6:
                e, part = divmod(i - 2, 2)
                c0 = PAIR_DIM + e * GROUP_RAW + part * half
                raw_a[2 * q + e, CARRY_ROWS:CARRY_ROWS + tm, part * half:(part + 1) * half] = _dot(
                    hn, wpair_ref[q, :, c0:c0 + half])
            else:
                part = i - 6
                acc_scr[...] += _dot(y_c[q, :, part * half:(part + 1) * half],
                                     wout_ref[q, part * half:(part + 1) * half, :])

        def step_sizes():
            dtt = jax.nn.softplus(_dot_nt(wdtt_ref[heads, :], hn) + dtb_ref[heads, :])
            dat = dtt * (-jnp.exp(alog_ref[heads, :]))
            cst = jnp.concatenate(
                [sum(_dot(p, upper) for p in _split3(dat[:, c * CHUNK:(c + 1) * CHUNK]))
                 for c in range(n_chunks)], axis=1)
            dtt_a[heads, :] = dtt
            cst_a[heads, :] = cst
            return dtt, cst

        def decays_by_token(dtt, cst):
            for c in range(n_chunks):
                r0, r1 = c * CHUNK, (c + 1) * CHUNK
                cst_c = cst[:, r0:r1]
                dte_c = jnp.exp(cst_c[:, CHUNK - 1:CHUNK] - cst_c) * dtt[:, r0:r1]
                pad = jnp.zeros((CHUNK - 3 * PAIR_HEADS, CHUNK), F32)
                col_a[q, r0:r1, :] = jnp.concatenate([cst_c, dte_c, jnp.exp(cst_c), pad], axis=0).T

        def conv(g, c0, c1):
            acc = cbias_ref[g, :, c0:c1]
            for k in range(D_CONV):
                start = CARRY_ROWS - (D_CONV - 1) + k
                acc = acc + cw_ref[g, k:k + 1, c0:c1] * raw_b[g, start:start + tm, c0:c1]
            return _silu(acc)

        units = [(e, c) for e in range(2) for c in range(n_chunks)]
        rows_of = lambda c: slice(c * CHUNK, (c + 1) * CHUNK)
        cols_of = lambda e: slice(e * GROUP_DIM, (e + 1) * GROUP_DIM)
        slices = iter(range(8))

        projection_slice(next(slices))
        dtt_next, cst_next = step_sizes()
        xg, xg_bf, bg, cg = {}, {}, {}, {}
        for e in range(2):
            g = 2 * q + e
            xg[e] = conv(g, 0, GROUP_DIM)
            xg_bf[e] = xg[e].astype(BF16)
            projection_slice(next(slices))
            bg[e] = conv(g, GROUP_DIM, GROUP_DIM + D_STATE).astype(BF16)
            projection_slice(next(slices))
            cg[e] = conv(g, GROUP_DIM + D_STATE, GROUP_RAW).astype(BF16)
            if e == 0:
                decays_by_token(dtt_next, cst_next)

        by_token = {c: col_b[q, rows_of(c), :] for c in range(n_chunks)}
        cst_c = {c: cst_b[heads, rows_of(c)] for c in range(n_chunks)}
        dtt_c = {c: dtt_b[heads, rows_of(c)] for c in range(n_chunks)}
        cb, dte_x, ecs_x, cd_x = {}, {}, {}, {}
        for e, c in units:
            expand = expand_ref[:, cols_of(e)]
            cb[e, c] = _dot_nt(cg[e][rows_of(c)], bg[e][rows_of(c)])
            dte_x[e, c] = _dot(by_token[c][:, PAIR_HEADS:2 * PAIR_HEADS].astype(BF16), expand)
            ecs_col = by_token[c][:, 2 * PAIR_HEADS:3 * PAIR_HEADS]
            ecs_x[e, c] = _dot(ecs_col.astype(BF16), expand)
            cd_hi, cd_mid, _ = _split3(ecs_col[CHUNK - CARRY_ROWS:, :])
            cd_x[e, c] = (_dot(cd_hi, expand) + _dot(cd_mid, expand))[CARRY_ROWS - 1:CARRY_ROWS, :]

        w_ls = {}
        for e, c in units:
            for j in range(HEADS_PER_GROUP):
                hd = e * HEADS_PER_GROUP + j
                seg = by_token[c][:, hd:hd + 1] - cst_c[c][hd:hd + 1, :]
                decay = jnp.exp(jnp.where(causal, seg, -jnp.inf))
                w_ls[e, c, j] = (cb[e, c] * decay * dtt_c[c][hd:hd + 1, :]).astype(BF16)
        projection_slice(next(slices))

        y_off = {}
        for e in range(2):
            g = 2 * q + e
            state = st_scr[g]
            if seq_start is not None:
                state = jnp.where(seq_start, 0.0, state)
            for c in range(n_chunks):
                y_off[e, c] = _dot(cg[e][rows_of(c)], state.astype(BF16)) * ecs_x[e, c]
                xw = (xg[e][rows_of(c)] * dte_x[e, c]).astype(BF16)
                state = state * cd_x[e, c] + _dot_tn(bg[e][rows_of(c)], xw)
            st_scr[g] = state

        def diag(e, c):
            return jnp.concatenate(
                [_dot(w_ls[e, c, j], xg_bf[e][rows_of(c), j * HEADDIM:(j + 1) * HEADDIM])
                 for j in range(HEADS_PER_GROUP)], axis=1)

        def finish(e, c, y_diag):
            g = 2 * q + e
            y = y_diag + y_off[e, c] + xg[e][rows_of(c)] * dskip_ref[g]
            gated = y * _silu(z_b[q, rows_of(c), cols_of(e)])
            y_b[q, rows_of(c), cols_of(e)] = _rms_rows(gated, gnw_ref[g]).astype(BF16)

        yd = [diag(*units[0]), diag(*units[1])]
        finish(*units[0], yd[0])
        yd.append(diag(*units[2]))
        projection_slice(next(slices))
        finish(*units[1], yd[1])
        yd.append(diag(*units[3]))
        projection_slice(next(slices))
        finish(*units[2], yd[2])
        finish(*units[3], yd[3])
        return carry

    hns, raws, zs, ys = (hn0, hn1), (raw0, raw1), (z0, z1), (y0, y1)
    dtts, csts, cols = (dtt0, dtt1), (cst0, cst1), (col0, col1)
    for sub in range(2):
        cur, prv = sub, 1 - sub
        rows = slice(sub * tm, (sub + 1) * tm)
        seq_start = (step % pairs_per_seq == 0) if sub == 1 else None
        tail = raws[cur][:, tm:tm + CARRY_ROWS, :]
        if seq_start is not None:
            tail = jnp.where(seq_start, 0.0, tail)
        raws[prv][:, 0:CARRY_ROWS, :] = tail
        hns[cur][...] = _rms_rows(h_ref[0, rows, :], nw_ref[...]).astype(BF16)
        acc_scr[...] = hres_ref[0, rows, :]
        lax.fori_loop(0, N_PAIRS, functools.partial(
            pair_body, hn_a=hns[cur], raw_a=raws[cur], z_a=zs[cur], dtt_a=dtts[cur], cst_a=csts[cur],
            col_a=cols[cur], raw_b=raws[prv], z_b=zs[prv], dtt_b=dtts[prv], cst_b=csts[prv],
            col_b=cols[prv], y_b=ys[prv], y_c=ys[cur], seq_start=seq_start), 0)
        out = acc_scr[...]
        if final_norm:
            out = _rms_rows(out, fnw_ref[...])
        o_ref[0, rows, :] = out


def _ssd_layer(h, norm_w, w_in, conv_w, conv_b, dt_bias, a_log, d_skip, gn_w, w_out, final_norm_w):
    bsz, seq, _ = h.shape
    tm = TOKEN_TILE
    pairs_per_seq = seq // (2 * tm)
    n_steps = bsz * pairs_per_seq

    def cur_tiles(s):
        p = jnp.minimum(s, n_steps - 1)
        return (p // pairs_per_seq, p % pairs_per_seq, 0)

    def prev_tiles(s):
        p = jnp.maximum(s - 1, 0)
        return (p // pairs_per_seq, p % pairs_per_seq, 0)

    def by_group(a, lead):
        off = a.shape[1] - CONV_DIM
        xs = a[:, off:off + D_INNER].reshape(lead, N_GROUPS, GROUP_DIM)
        bs = a[:, off + D_INNER:off + D_INNER + BC_DIM].reshape(lead, N_GROUPS, D_STATE)
        cs = a[:, off + D_INNER + BC_DIM:].reshape(lead, N_GROUPS, D_STATE)
        return jnp.concatenate([xs, bs, cs], axis=2)

    w_zxbc = w_in[:, :D_INNER + CONV_DIM].astype(BF16)
    w_z = w_zxbc[:, :D_INNER].reshape(D_MODEL, N_PAIRS, PAIR_DIM)
    w_raw = by_group(w_zxbc, D_MODEL).reshape(D_MODEL, N_PAIRS, 2 * GROUP_RAW)
    w_pair = jnp.concatenate([w_z, w_raw], axis=2).transpose(1, 0, 2)
    w_dtt = w_in[:, D_INNER + CONV_DIM:].astype(BF16).T
    cw = by_group(conv_w, D_CONV).transpose(1, 0, 2)
    cbias = by_group(conv_b.reshape(1, CONV_DIM), 1).transpose(1, 0, 2)
    expand = jnp.repeat(jnp.eye(PAIR_HEADS, dtype=BF16), HEADDIM, axis=1)
    final = final_norm_w is not None
    fnw = final_norm_w if final else jnp.ones((D_MODEL,), F32)

    def const(shape, **kw):
        return pl.BlockSpec(shape, lambda s: (0,) * len(shape), **kw)

    resident = pl.Buffered(1)
    tile_pair = (1, 2 * tm, D_MODEL)
    return pl.pallas_call(
        functools.partial(_ssd_kernel, tm=tm, pairs_per_seq=pairs_per_seq, final_norm=final),
        grid=(n_steps + 1,),
        in_specs=[
            pl.BlockSpec(tile_pair, cur_tiles),
            pl.BlockSpec(tile_pair, prev_tiles),
            const((1, D_MODEL)),
            const((N_PAIRS, D_MODEL, PAIR_COLS), pipeline_mode=resident),
            const((N_HEADS, D_MODEL)),
            const((N_GROUPS, D_CONV, GROUP_RAW)),
            const((N_GROUPS, 1, GROUP_RAW)),
            const((N_HEADS, 1)),
            const((N_HEADS, 1)),
            const((N_GROUPS, 1, GROUP_DIM)),
            const((N_GROUPS, 1, GROUP_DIM)),
            const((PAIR_HEADS, PAIR_DIM)),
            const((N_PAIRS, PAIR_DIM, D_MODEL), pipeline_mode=resident),
            const((1, D_MODEL)),
        ],
        out_specs=pl.BlockSpec(tile_pair, prev_tiles),
        out_shape=jax.ShapeDtypeStruct(h.shape, F32),
        scratch_shapes=(
            [pltpu.VMEM((tm, D_MODEL), BF16)] * 2
            + [pltpu.VMEM((N_GROUPS, CARRY_ROWS + tm, GROUP_RAW), F32)] * 2
            + [pltpu.VMEM((N_PAIRS, tm, PAIR_DIM), F32)] * 2
            + [pltpu.VMEM((N_HEADS, tm), F32)] * 2
            + [pltpu.VMEM((N_HEADS, tm), F32)] * 2
            + [pltpu.VMEM((N_PAIRS, tm, CHUNK), F32)] * 2
            + [pltpu.VMEM((N_PAIRS, tm, PAIR_DIM), BF16)] * 2
            + [pltpu.VMEM((N_GROUPS, D_STATE, GROUP_DIM), F32)]
            + [pltpu.VMEM((tm, D_MODEL), F32)]
        ),
        compiler_params=pltpu.CompilerParams(
            dimension_semantics=("arbitrary",),
            vmem_limit_bytes=VMEM_LIMIT_BYTES),
        name="ssd_final" if final else "ssd_layer",
    )(h, h, norm_w.reshape(1, D_MODEL), w_pair, w_dtt, cw, cbias,
      dt_bias.reshape(N_HEADS, 1), a_log.reshape(N_HEADS, 1),
      jnp.repeat(d_skip, HEADDIM).reshape(N_GROUPS, 1, GROUP_DIM), gn_w.reshape(N_GROUPS, 1, GROUP_DIM),
      expand, w_out.astype(BF16).reshape(N_PAIRS, PAIR_DIM, D_MODEL), fnw.reshape(1, D_MODEL))


def kernel(x, norm_w, final_norm_w, a_w_in, a_ln_w, a_ln_b, a_w_s, a_b_s, a_w_out,
           b_w_in, b_conv_w, b_conv_b, b_dt_bias, b_a_log, b_d_skip, b_norm_w, b_w_out):
    depth = norm_w.shape[0]
    h = x
    for i in range(depth):
        k = i // 2
        if i % 2 == 0:
            h = _sgu_layer(h, norm_w[i], a_w_in[k], a_ln_w[k], a_ln_b[k], a_w_s[k], a_b_s[k],
                           a_w_out[k])
        else:
            h = _ssd_layer(h, norm_w[i], b_w_in[k], b_conv_w[k], b_conv_b[k], b_dt_bias[k],
                           b_a_log[k], b_d_skip[k], b_norm_w[k], b_w_out[k],
                           final_norm_w if i == depth - 1 else None)
    return h
```

```python
import functools

import jax
import jax.numpy as jnp
from jax import lax
from jax.experimental import pallas as pl
from jax.experimental.pallas import tpu as pltpu

D_MODEL = 1024
D_INNER = 2048
CHUNK = 128
N_GROUPS = 8
GROUP_DIM = D_INNER // N_GROUPS
HEADDIM = 64
HEADS_PER_GROUP = GROUP_DIM // HEADDIM
N_HEADS = D_INNER // HEADDIM
D_STATE = 128
D_CONV = 4
BC_DIM = N_GROUPS * D_STATE
CONV_DIM = D_INNER + 2 * BC_DIM
NORM_EPS = 1e-6
LN_EPS = 1e-5
CARRY_ROWS = 8

SGU_TOKEN_TILE = 512
SSD_TOKEN_TILE = 256
LANES = 128
VMEM_LIMIT_BYTES = 56 * 1024 * 1024

BF16 = jnp.bfloat16
F32 = jnp.float32


def _dot(a, b):
    return jnp.dot(a, b, preferred_element_type=F32)


def _dot_nt(a, b):
    return lax.dot_general(a, b, (((1,), (1,)), ((), ())), preferred_element_type=F32)


def _dot_tn(a, b):
    return lax.dot_general(a, b, (((0,), (0,)), ((), ())), preferred_element_type=F32)


def _silu(x):
    return x * (1.0 / (1.0 + jnp.exp(-x)))


def _rms_rows(x, w):
    return x * lax.rsqrt(jnp.mean(x * x, axis=-1, keepdims=True) + NORM_EPS) * w


def _split3(a):
    hi = a.astype(BF16)
    r1 = a - hi.astype(F32)
    mid = r1.astype(BF16)
    lo = (r1 - mid.astype(F32)).astype(BF16)
    return hi, mid, lo


def _sgu_kernel(h_ref, nw_ref, win_ref, lnw_ref, lnb_ref, ws_ref, bst_ref, wout_ref,
                o_ref, v_scr, g_scr, *, tm):
    x = h_ref[0]
    hn = _rms_rows(x, nw_ref[...]).astype(BF16)

    v = _dot(hn, win_ref[:, 2 * D_INNER:3 * D_INNER])
    mu = jnp.mean(v, axis=-1, keepdims=True)
    vc = v - mu
    var = jnp.mean(vc * vc, axis=-1, keepdims=True)
    vn = vc * lax.rsqrt(var + LN_EPS) * lnw_ref[...] + lnb_ref[...]
    v_scr[...] = vn.astype(BF16)

    row = lax.broadcasted_iota(jnp.int32, (CHUNK, CHUNK), 0)
    col = lax.broadcasted_iota(jnp.int32, (CHUNK, CHUNK), 1)
    causal = row >= col
    for g in range(N_GROUPS):
        lo, hi = g * GROUP_DIM, (g + 1) * GROUP_DIM
        z = _dot(hn, win_ref[:, lo:hi])
        u = _dot(hn, win_ref[:, D_INNER + lo:D_INNER + hi])
        gate = u * _silu(z)
        wc = jnp.where(causal, ws_ref[g], 0.0).astype(BF16)
        bias = bst_ref[:, g:g + 1]
        for c in range(tm // CHUNK):
            r0, r1 = c * CHUNK, (c + 1) * CHUNK
            mixed = _dot(wc, v_scr[r0:r1, lo:hi]) + bias
            g_scr[r0:r1, lo:hi] = (gate[r0:r1] * mixed).astype(BF16)

    o_ref[0] = _dot(g_scr[...], wout_ref[...]) + x


def _sgu_layer(h, norm_w, w_in, ln_w, ln_b, w_s, b_s, w_out):
    bsz, seq, _ = h.shape
    tm = SGU_TOKEN_TILE
    const2 = lambda b, i: (0, 0)
    const3 = lambda b, i: (0, 0, 0)
    tile = lambda b, i: (b, i, 0)
    resident = pl.Buffered(1)
    return pl.pallas_call(
        functools.partial(_sgu_kernel, tm=tm),
        grid=(bsz, seq // tm),
        in_specs=[
            pl.BlockSpec((1, tm, D_MODEL), tile),
            pl.BlockSpec((1, D_MODEL), const2),
            pl.BlockSpec((D_MODEL, 3 * D_INNER), const2, pipeline_mode=resident),
            pl.BlockSpec((1, D_INNER), const2),
            pl.BlockSpec((1, D_INNER), const2),
            pl.BlockSpec((N_GROUPS, CHUNK, CHUNK), const3),
            pl.BlockSpec((CHUNK, N_GROUPS), const2),
            pl.BlockSpec((D_INNER, D_MODEL), const2, pipeline_mode=resident),
        ],
        out_specs=pl.BlockSpec((1, tm, D_MODEL), tile),
        out_shape=jax.ShapeDtypeStruct(h.shape, F32),
        scratch_shapes=[
            pltpu.VMEM((tm, D_INNER), BF16),
            pltpu.VMEM((tm, D_INNER), BF16),
        ],
        compiler_params=pltpu.CompilerParams(
            dimension_semantics=("arbitrary", "arbitrary"),
            vmem_limit_bytes=VMEM_LIMIT_BYTES),
        name="sgu_layer",
    )(h, norm_w.reshape(1, D_MODEL), w_in.astype(BF16), ln_w.reshape(1, D_INNER),
      ln_b.reshape(1, D_INNER), w_s, b_s.T, w_out.astype(BF16))


def _ssd_kernel(h_ref, nw_ref, wzx_ref, wdt_ref, wdtt_ref, cw_ref, cbias_ref, dtb_ref, dtbt_ref,
                alog_ref, alogt_ref, dskip_ref, gnw_ref, expand_ref, wout_ref, fnw_ref,
                o_ref, raw_scr, st_scr, y_scr, *, tm, final_norm):
    @pl.when(pl.program_id(1) == 0)
    def _():
        st_scr[...] = jnp.zeros_like(st_scr)
        raw_scr[:, 0:CARRY_ROWS, :] = jnp.zeros((CONV_DIM // LANES, CARRY_ROWS, LANES), F32)

    x_in = h_ref[0]
    hn = _rms_rows(x_in, nw_ref[...]).astype(BF16)

    raw = _dot(hn, wzx_ref[:, D_INNER:])
    for n in range(CONV_DIM // LANES):
        raw_scr[n, CARRY_ROWS:CARRY_ROWS + tm, :] = raw[:, n * LANES:(n + 1) * LANES]

    dt = jax.nn.softplus(_dot(hn, wdt_ref[...]) + dtb_ref[...])
    dtt = jax.nn.softplus(_dot_nt(wdtt_ref[...], hn) + dtbt_ref[...])
    da = dt * (-jnp.exp(alog_ref[...]))
    dat = dtt * (-jnp.exp(alogt_ref[...]))

    row = lax.broadcasted_iota(jnp.int32, (CHUNK, CHUNK), 0)
    col = lax.broadcasted_iota(jnp.int32, (CHUNK, CHUNK), 1)
    causal = row >= col
    lower = causal.astype(BF16)
    upper = (row <= col).astype(BF16)

    n_chunks = tm // CHUNK
    cs, cst, dte, ecs, cdec = [], [], [], [], []
    for c in range(n_chunks):
        r0, r1 = c * CHUNK, (c + 1) * CHUNK
        cs_c = sum(_dot(lower, p) for p in _split3(da[r0:r1]))
        cst_c = sum(_dot(p, upper) for p in _split3(dat[:, r0:r1]))
        last = cs_c[CHUNK - 1:CHUNK, :]
        cs.append(cs_c)
        cst.append(cst_c - jnp.log(dtt[:, r0:r1]))
        dte.append(jnp.exp(last - cs_c) * dt[r0:r1])
        ecs.append(jnp.exp(cs_c))
        cdec.append(jnp.broadcast_to(jnp.exp(last), (CARRY_ROWS, N_HEADS)))

    for g in range(N_GROUPS):
        lo, hi = g * GROUP_DIM, (g + 1) * GROUP_DIM
        blo, bhi = D_INNER + g * D_STATE, D_INNER + (g + 1) * D_STATE
        clo, chi = blo + BC_DIM, bhi + BC_DIM
        expand_g = expand_ref[:, lo:hi]

        def conv(c0, c1):
            slabs = []
            for n in range(c0 // LANES, c1 // LANES):
                acc = cbias_ref[:, n * LANES:(n + 1) * LANES]
                for k in range(D_CONV):
                    start = CARRY_ROWS - (D_CONV - 1) + k
                    tap = raw_scr[n, pl.ds(start, tm, stride=1), :]
                    acc = acc + cw_ref[k:k + 1, n * LANES:(n + 1) * LANES] * tap
                slabs.append(_silu(acc))
            return jnp.concatenate(slabs, axis=1)

        xg = conv(lo, hi)
        bg = conv(blo, bhi).astype(BF16)
        cg = conv(clo, chi).astype(BF16)
        zg = _dot(hn, wzx_ref[:, lo:hi])
        xg_bf = xg.astype(BF16)

        for c in range(n_chunks):
            r0, r1 = c * CHUNK, (c + 1) * CHUNK
            b_c, c_c, x_c = bg[r0:r1], cg[r0:r1], xg_bf[r0:r1]
            cb = _dot_nt(c_c, b_c)
            y_heads = []
            for j in range(HEADS_PER_GROUP):
                hd = g * HEADS_PER_GROUP + j
                seg = cs[c][:, hd:hd + 1] - cst[c][hd:hd + 1, :]
                decay_dt = jnp.exp(jnp.where(causal, seg, -jnp.inf))
                w_ls = (cb * decay_dt).astype(BF16)
                y_heads.append(_dot(w_ls, x_c[:, j * HEADDIM:(j + 1) * HEADDIM]))
            y_diag = jnp.concatenate(y_heads, axis=1)

            dte_x = _dot(dte[c].astype(BF16), expand_g)
            ecs_x = _dot(ecs[c].astype(BF16), expand_g)
            cd_hi, cd_mid, _ = _split3(cdec[c])
            cd_x = (_dot(cd_hi, expand_g) + _dot(cd_mid, expand_g))[0:1, :]

            state = st_scr[g]
            y_off = _dot(c_c, state.astype(BF16)) * ecs_x
            xw = (xg[r0:r1] * dte_x).astype(BF16)
            st_scr[g] = state * cd_x + _dot_tn(b_c, xw)

            y = y_diag + y_off + xg[r0:r1] * dskip_ref[:, lo:hi]
            gated = y * _silu(zg[r0:r1])
            y_scr[r0:r1, lo:hi] = _rms_rows(gated, gnw_ref[:, lo:hi]).astype(BF16)

    raw_scr[:, 0:CARRY_ROWS, :] = raw_scr[:, tm:tm + CARRY_ROWS, :]

    out = _dot(y_scr[...], wout_ref[...]) + x_in
    if final_norm:
        out = _rms_rows(out, fnw_ref[...])
    o_ref[0] = out


def _ssd_layer(h, norm_w, w_in, conv_w, conv_b, dt_bias, a_log, d_skip, gn_w, w_out, final_norm_w):
    bsz, seq, _ = h.shape
    tm = SSD_TOKEN_TILE
    const2 = lambda b, i: (0, 0)
    tile = lambda b, i: (b, i, 0)
    resident = pl.Buffered(1)
    w_zx = w_in[:, :D_INNER + CONV_DIM].astype(BF16)
    w_dt = w_in[:, D_INNER + CONV_DIM:].astype(BF16)
    expand = jnp.repeat(jnp.eye(N_HEADS, dtype=BF16), HEADDIM, axis=1)
    final = final_norm_w is not None
    fnw = final_norm_w if final else jnp.ones((D_MODEL,), F32)
    small = lambda shape: pl.BlockSpec(shape, const2)
    return pl.pallas_call(
        functools.partial(_ssd_kernel, tm=tm, final_norm=final),
        grid=(bsz, seq // tm),
        in_specs=[
            pl.BlockSpec((1, tm, D_MODEL), tile),
            small((1, D_MODEL)),
            pl.BlockSpec((D_MODEL, D_INNER + CONV_DIM), const2, pipeline_mode=resident),
            small((D_MODEL, N_HEADS)),
            small((N_HEADS, D_MODEL)),
            small((D_CONV, CONV_DIM)),
            small((1, CONV_DIM)),
            small((1, N_HEADS)),
            small((N_HEADS, 1)),
            small((1, N_HEADS)),
            small((N_HEADS, 1)),
            small((1, D_INNER)),
            small((1, D_INNER)),
            small((N_HEADS, D_INNER)),
            pl.BlockSpec((D_INNER, D_MODEL), const2, pipeline_mode=resident),
            small((1, D_MODEL)),
        ],
        out_specs=pl.BlockSpec((1, tm, D_MODEL), tile),
        out_shape=jax.ShapeDtypeStruct(h.shape, F32),
        scratch_shapes=[
            pltpu.VMEM((CONV_DIM // LANES, CARRY_ROWS + tm, LANES), F32),
            pltpu.VMEM((N_GROUPS, D_STATE, GROUP_DIM), F32),
            pltpu.VMEM((tm, D_INNER), BF16),
        ],
        compiler_params=pltpu.CompilerParams(
            dimension_semantics=("arbitrary", "arbitrary"),
            vmem_limit_bytes=VMEM_LIMIT_BYTES),
        name="ssd_final" if final else "ssd_layer",
    )(h, norm_w.reshape(1, D_MODEL), w_zx, w_dt, w_dt.T, conv_w, conv_b.reshape(1, CONV_DIM),
      dt_bias.reshape(1, N_HEADS), dt_bias.reshape(N_HEADS, 1),
      a_log.reshape(1, N_HEADS), a_log.reshape(N_HEADS, 1),
      jnp.repeat(d_skip, HEADDIM).reshape(1, D_INNER), gn_w.reshape(1, D_INNER),
      expand, w_out.astype(BF16), fnw.reshape(1, D_MODEL))


def kernel(x, norm_w, final_norm_w, a_w_in, a_ln_w, a_ln_b, a_w_s, a_b_s, a_w_out,
           b_w_in, b_conv_w, b_conv_b, b_dt_bias, b_a_log, b_d_skip, b_norm_w, b_w_out):
    depth = norm_w.shape[0]
    h = x
    for i in range(depth):
        k = i // 2
        if i % 2 == 0:
            h = _sgu_layer(h, norm_w[i], a_w_in[k], a_ln_w[k], a_ln_b[k], a_w_s[k], a_b_s[k],
                           a_w_out[k])
        else:
            h = _ssd_layer(h, norm_w[i], b_w_in[k], b_conv_w[k], b_conv_b[k], b_dt_bias[k],
                           b_a_log[k], b_d_skip[k], b_norm_w[k], b_w_out[k],
                           final_norm_w if i == depth - 1 else None)
    return h
```

```python
import functools

import jax
import jax.numpy as jnp
from jax import lax
from jax.experimental import pallas as pl
from jax.experimental.pallas import tpu as pltpu

D_MODEL = 1024
D_INNER = 2048
CHUNK = 128
N_GROUPS = 8
GROUP_DIM = D_INNER // N_GROUPS
HEADDIM = 64
HEADS_PER_GROUP = GROUP_DIM // HEADDIM
N_HEADS = D_INNER // HEADDIM
D_STATE = 128
D_CONV = 4
BC_DIM = N_GROUPS * D_STATE
CONV_DIM = D_INNER + 2 * BC_DIM
NORM_EPS = 1e-6
LN_EPS = 1e-5
CARRY_ROWS = 8

SGU_TOKEN_TILE = 512
SSD_TOKEN_TILE = 256
LANES = 128
LOG2E = 1.4426950408889634
VMEM_LIMIT_BYTES = 56 * 1024 * 1024

BF16 = jnp.bfloat16
F32 = jnp.float32


def _dot(a, b):
    return jnp.dot(a, b, preferred_element_type=F32)


def _dot_nt(a, b):
    return lax.dot_general(a, b, (((1,), (1,)), ((), ())), preferred_element_type=F32)


def _dot_tn(a, b):
    return lax.dot_general(a, b, (((0,), (0,)), ((), ())), preferred_element_type=F32)


def _silu(x):
    half = 0.5 * x
    return half + half * jnp.tanh(half)


def _rms_rows(x, w):
    return x * lax.rsqrt(jnp.mean(x * x, axis=-1, keepdims=True) + NORM_EPS) * w


def _split3(a):
    hi = a.astype(BF16)
    r1 = a - hi.astype(F32)
    mid = r1.astype(BF16)
    lo = (r1 - mid.astype(F32)).astype(BF16)
    return hi, mid, lo


def _sgu_kernel(h_ref, nw_ref, win_ref, lnw_ref, lnb_ref, ws_ref, bst_ref, wout_ref,
                o_ref, v_scr, g_scr, *, tm):
    x = h_ref[0]
    hn = _rms_rows(x, nw_ref[...]).astype(BF16)

    v = _dot(hn, win_ref[:, 2 * D_INNER:3 * D_INNER])
    mu = jnp.mean(v, axis=-1, keepdims=True)
    vc = v - mu
    var = jnp.mean(vc * vc, axis=-1, keepdims=True)
    vn = vc * lax.rsqrt(var + LN_EPS) * lnw_ref[...] + lnb_ref[...]
    v_scr[...] = vn.astype(BF16)

    row = lax.broadcasted_iota(jnp.int32, (CHUNK, CHUNK), 0)
    col = lax.broadcasted_iota(jnp.int32, (CHUNK, CHUNK), 1)
    causal = row >= col
    for g in range(N_GROUPS):
        lo, hi = g * GROUP_DIM, (g + 1) * GROUP_DIM
        z = _dot(hn, win_ref[:, lo:hi])
        u = _dot(hn, win_ref[:, D_INNER + lo:D_INNER + hi])
        gate = u * _silu(z)
        wc = jnp.where(causal, ws_ref[g], 0.0).astype(BF16)
        bias = bst_ref[:, g:g + 1]
        for c in range(tm // CHUNK):
            r0, r1 = c * CHUNK, (c + 1) * CHUNK
            mixed = _dot(wc, v_scr[r0:r1, lo:hi]) + bias
            g_scr[r0:r1, lo:hi] = (gate[r0:r1] * mixed).astype(BF16)

    o_ref[0] = _dot(g_scr[...], wout_ref[...]) + x


def _sgu_layer(h, norm_w, w_in, ln_w, ln_b, w_s, b_s, w_out):
    bsz, seq, _ = h.shape
    tm = SGU_TOKEN_TILE
    const2 = lambda b, i: (0, 0)
    const3 = lambda b, i: (0, 0, 0)
    tile = lambda b, i: (b, i, 0)
    resident = pl.Buffered(1)
    return pl.pallas_call(
        functools.partial(_sgu_kernel, tm=tm),
        grid=(bsz, seq // tm),
        in_specs=[
            pl.BlockSpec((1, tm, D_MODEL), tile),
            pl.BlockSpec((1, D_MODEL), const2),
            pl.BlockSpec((D_MODEL, 3 * D_INNER), const2, pipeline_mode=resident),
            pl.BlockSpec((1, D_INNER), const2),
            pl.BlockSpec((1, D_INNER), const2),
            pl.BlockSpec((N_GROUPS, CHUNK, CHUNK), const3),
            pl.BlockSpec((CHUNK, N_GROUPS), const2),
            pl.BlockSpec((D_INNER, D_MODEL), const2, pipeline_mode=resident),
        ],
        out_specs=pl.BlockSpec((1, tm, D_MODEL), tile),
        out_shape=jax.ShapeDtypeStruct(h.shape, F32),
        scratch_shapes=[
            pltpu.VMEM((tm, D_INNER), BF16),
            pltpu.VMEM((tm, D_INNER), BF16),
        ],
        compiler_params=pltpu.CompilerParams(
            dimension_semantics=("arbitrary", "arbitrary"),
            vmem_limit_bytes=VMEM_LIMIT_BYTES),
        name="sgu_layer",
    )(h, norm_w.reshape(1, D_MODEL), w_in.astype(BF16), ln_w.reshape(1, D_INNER),
      ln_b.reshape(1, D_INNER), w_s, b_s.T, w_out.astype(BF16))


def _ssd_kernel(h_ref, nw_ref, wzx_ref, wdt_ref, wdtt_ref, cw_ref, cbias_ref, dtb_ref, dtbt_ref,
                alog_ref, alogt_ref, dskip_ref, gnw_ref, wout_ref, fnw_ref,
                o_ref, raw_scr, st_scr, y_scr, *, tm, final_norm):
    @pl.when(pl.program_id(1) == 0)
    def _():
        st_scr[...] = jnp.zeros_like(st_scr)
        raw_scr[:, 0:CARRY_ROWS, :] = jnp.zeros((CONV_DIM // LANES, CARRY_ROWS, LANES), F32)

    x_in = h_ref[0]
    hn = _rms_rows(x_in, nw_ref[...]).astype(BF16)

    raw = _dot(hn, wzx_ref[:, D_INNER:])
    for n in range(CONV_DIM // LANES):
        raw_scr[n, CARRY_ROWS:CARRY_ROWS + tm, :] = raw[:, n * LANES:(n + 1) * LANES]

    dt = jax.nn.softplus(_dot(hn, wdt_ref[...]) + dtb_ref[...])
    dtt = jax.nn.softplus(_dot_nt(wdtt_ref[...], hn) + dtbt_ref[...])
    da = dt * (-jnp.exp(alog_ref[...]))
    dat = dtt * (-jnp.exp(alogt_ref[...]))

    row = lax.broadcasted_iota(jnp.int32, (CHUNK, CHUNK), 0)
    col = lax.broadcasted_iota(jnp.int32, (CHUNK, CHUNK), 1)
    causal = row >= col
    first_head = col < HEADDIM
    lower = causal.astype(BF16)
    upper = (row <= col).astype(BF16)

    n_chunks = tm // CHUNK
    cs, cst, dte = [], [], []
    for c in range(n_chunks):
        r0, r1 = c * CHUNK, (c + 1) * CHUNK
        cs_c = sum(_dot(lower, p) for p in _split3(da[r0:r1]))
        cst_c = sum(_dot(p, upper) for p in _split3(dat[:, r0:r1]))
        last = cs_c[CHUNK - 1:CHUNK, :]
        cs.append(cs_c * LOG2E)
        cst.append((cst_c - jnp.log(dtt[:, r0:r1])) * LOG2E)
        dte.append((last - (cs_c - jnp.log(dt[r0:r1]))) * LOG2E)

    for g in range(N_GROUPS):
        lo, hi = g * GROUP_DIM, (g + 1) * GROUP_DIM
        blo, bhi = D_INNER + g * D_STATE, D_INNER + (g + 1) * D_STATE
        clo, chi = blo + BC_DIM, bhi + BC_DIM

        def conv(c0, c1):
            slabs = []
            for n in range(c0 // LANES, c1 // LANES):
                acc = cbias_ref[:, n * LANES:(n + 1) * LANES]
                for k in range(D_CONV):
                    start = CARRY_ROWS - (D_CONV - 1) + k
                    tap = raw_scr[n, pl.ds(start, tm, stride=1), :]
                    acc = acc + cw_ref[k:k + 1, n * LANES:(n + 1) * LANES] * tap
                slabs.append(_silu(acc))
            return jnp.concatenate(slabs, axis=1)

        xg = conv(lo, hi)
        bg = conv(blo, bhi).astype(BF16)
        cg = conv(clo, chi).astype(BF16)
        zg = _dot(hn, wzx_ref[:, lo:hi])
        xg_bf = xg.astype(BF16)

        for c in range(n_chunks):
            r0, r1 = c * CHUNK, (c + 1) * CHUNK
            b_c, c_c, x_c = bg[r0:r1], cg[r0:r1], xg_bf[r0:r1]
            cb = _dot_nt(c_c, b_c)
            y_heads, ecs_b, dte_b = [], [], []
            for j in range(HEADS_PER_GROUP):
                hd = g * HEADS_PER_GROUP + j
                cs_l = jnp.broadcast_to(cs[c][:, hd:hd + 1], (CHUNK, CHUNK))
                seg = cs_l - cst[c][hd:hd + 1, :]
                decay_dt = jnp.exp2(jnp.where(causal, seg, -jnp.inf))
                w_ls = (cb * decay_dt).astype(BF16)
                y_heads.append(_dot(w_ls, x_c[:, j * HEADDIM:(j + 1) * HEADDIM]))
                ecs_b.append(jnp.exp2(cs_l))
                dte_b.append(jnp.exp2(jnp.broadcast_to(dte[c][:, hd:hd + 1], (CHUNK, CHUNK))))
            y_diag = jnp.concatenate(y_heads, axis=1)

            ecs_x = jnp.concatenate([jnp.where(first_head, ecs_b[j], ecs_b[j + 1])
                                     for j in range(0, HEADS_PER_GROUP, 2)], axis=1)
            dte_x = jnp.concatenate([jnp.where(first_head, dte_b[j], dte_b[j + 1])
                                     for j in range(0, HEADS_PER_GROUP, 2)], axis=1)
            cd_x = ecs_x[CHUNK - 1:CHUNK, :]

            state = st_scr[g]
            y_off = _dot(c_c, state.astype(BF16)) * ecs_x
            xw = (xg[r0:r1] * dte_x).astype(BF16)
            st_scr[g] = state * cd_x + _dot_tn(b_c, xw)

            y = y_diag + y_off + xg[r0:r1] * dskip_ref[:, lo:hi]
            gated = y * _silu(zg[r0:r1])
            y_scr[r0:r1, lo:hi] = _rms_rows(gated, gnw_ref[:, lo:hi]).astype(BF16)

    raw_scr[:, 0:CARRY_ROWS, :] = raw_scr[:, tm:tm + CARRY_ROWS, :]

    out = _dot(y_scr[...], wout_ref[...]) + x_in
    if final_norm:
        out = _rms_rows(out, fnw_ref[...])
    o_ref[0] = out


def _ssd_layer(h, norm_w, w_in, conv_w, conv_b, dt_bias, a_log, d_skip, gn_w, w_out, final_norm_w):
    bsz, seq, _ = h.shape
    tm = SSD_TOKEN_TILE
    const2 = lambda b, i: (0, 0)
    tile = lambda b, i: (b, i, 0)
    resident = pl.Buffered(1)
    w_zx = w_in[:, :D_INNER + CONV_DIM].astype(BF16)
    w_dt = w_in[:, D_INNER + CONV_DIM:].astype(BF16)
    final = final_norm_w is not None
    fnw = final_norm_w if final else jnp.ones((D_MODEL,), F32)
    small = lambda shape: pl.BlockSpec(shape, const2)
    return pl.pallas_call(
        functools.partial(_ssd_kernel, tm=tm, final_norm=final),
        grid=(bsz, seq // tm),
        in_specs=[
            pl.BlockSpec((1, tm, D_MODEL), tile),
            small((1, D_MODEL)),
            pl.BlockSpec((D_MODEL, D_INNER + CONV_DIM), const2, pipeline_mode=resident),
            small((D_MODEL, N_HEADS)),
            small((N_HEADS, D_MODEL)),
            small((D_CONV, CONV_DIM)),
            small((1, CONV_DIM)),
            small((1, N_HEADS)),
            small((N_HEADS, 1)),
            small((1, N_HEADS)),
            small((N_HEADS, 1)),
            small((1, D_INNER)),
            small((1, D_INNER)),
            pl.BlockSpec((D_INNER, D_MODEL), const2, pipeline_mode=resident),
            small((1, D_MODEL)),
        ],
        out_specs=pl.BlockSpec((1, tm, D_MODEL), tile),
        out_shape=jax.ShapeDtypeStruct(h.shape, F32),
        scratch_shapes=[
            pltpu.VMEM((CONV_DIM // LANES, CARRY_ROWS + tm, LANES), F32),
            pltpu.VMEM((N_GROUPS, D_STATE, GROUP_DIM), F32),
            pltpu.VMEM((tm, D_INNER), BF16),
        ],
        compiler_params=pltpu.CompilerParams(
            dimension_semantics=("arbitrary", "arbitrary"),
            vmem_limit_bytes=VMEM_LIMIT_BYTES),
        name="ssd_final" if final else "ssd_layer",
    )(h, norm_w.reshape(1, D_MODEL), w_zx, w_dt, w_dt.T, conv_w, conv_b.reshape(1, CONV_DIM),
      dt_bias.reshape(1, N_HEADS), dt_bias.reshape(N_HEADS, 1),
      a_log.reshape(1, N_HEADS), a_log.reshape(N_HEADS, 1),
      jnp.repeat(d_skip, HEADDIM).reshape(1, D_INNER), gn_w.reshape(1, D_INNER),
      w_out.astype(BF16), fnw.reshape(1, D_MODEL))


def kernel(x, norm_w, final_norm_w, a_w_in, a_ln_w, a_ln_b, a_w_s, a_b_s, a_w_out,
           b_w_in, b_conv_w, b_conv_b, b_dt_bias, b_a_log, b_d_skip, b_norm_w, b_w_out):
    depth = norm_w.shape[0]
    h = x
    for i in range(depth):
        k = i // 2
        if i % 2 == 0:
            h = _sgu_layer(h, norm_w[i], a_w_in[k], a_ln_w[k], a_ln_b[k], a_w_s[k], a_b_s[k],
                           a_w_out[k])
        else:
            h = _ssd_layer(h, norm_w[i], b_w_in[k], b_conv_w[k], b_conv_b[k], b_dt_bias[k],
                           b_a_log[k], b_d_skip[k], b_norm_w[k], b_w_out[k],
                           final_norm_w if i == depth - 1 else None)
    return h
```

```python
import functools

import jax
import jax.numpy as jnp
from jax import lax
from jax.experimental import pallas as pl
from jax.experimental.pallas import tpu as pltpu

D_MODEL = 1024
D_INNER = 2048
CHUNK = 128
N_GROUPS = 8
GROUP_DIM = D_INNER // N_GROUPS
HEADDIM = 64
HEADS_PER_GROUP = GROUP_DIM // HEADDIM
N_HEADS = D_INNER // HEADDIM
D_STATE = 128
D_CONV = 4
BC_DIM = N_GROUPS * D_STATE
CONV_DIM = D_INNER + 2 * BC_DIM
NORM_EPS = 1e-6
LN_EPS = 1e-5
CARRY_ROWS = 8

SGU_TOKEN_TILE = 512
SSD_TOKEN_TILE = 256
LANES = 128
LOG2E = 1.4426950408889634
VMEM_LIMIT_BYTES = 56 * 1024 * 1024

BF16 = jnp.bfloat16
F32 = jnp.float32


def _dot(a, b):
    return jnp.dot(a, b, preferred_element_type=F32)


def _dot_nt(a, b):
    return lax.dot_general(a, b, (((1,), (1,)), ((), ())), preferred_element_type=F32)


def _dot_tn(a, b):
    return lax.dot_general(a, b, (((0,), (0,)), ((), ())), preferred_element_type=F32)


def _silu(x):
    half = 0.5 * x
    return half + half * jnp.tanh(half)


def _rms_rows(x, w):
    return x * lax.rsqrt(jnp.mean(x * x, axis=-1, keepdims=True) + NORM_EPS) * w


def _split3(a):
    hi = a.astype(BF16)
    r1 = a - hi.astype(F32)
    mid = r1.astype(BF16)
    lo = (r1 - mid.astype(F32)).astype(BF16)
    return hi, mid, lo


def _sgu_kernel(h_ref, nw_ref, win_ref, lnw_ref, lnb_ref, ws_ref, bst_ref, wout_ref,
                o_ref, v_scr, g_scr, *, tm):
    x = h_ref[0]
    hn = _rms_rows(x, nw_ref[...]).astype(BF16)

    v = _dot(hn, win_ref[:, 2 * D_INNER:3 * D_INNER])
    mu = jnp.mean(v, axis=-1, keepdims=True)
    vc = v - mu
    var = jnp.mean(vc * vc, axis=-1, keepdims=True)
    vn = vc * lax.rsqrt(var + LN_EPS) * lnw_ref[...] + lnb_ref[...]
    v_scr[...] = vn.astype(BF16)

    row = lax.broadcasted_iota(jnp.int32, (CHUNK, CHUNK), 0)
    col = lax.broadcasted_iota(jnp.int32, (CHUNK, CHUNK), 1)
    causal = row >= col
    for g in range(N_GROUPS):
        lo, hi = g * GROUP_DIM, (g + 1) * GROUP_DIM
        z = _dot(hn, win_ref[:, lo:hi])
        u = _dot(hn, win_ref[:, D_INNER + lo:D_INNER + hi])
        gate = u * _silu(z)
        wc = jnp.where(causal, ws_ref[g], 0.0).astype(BF16)
        bias = bst_ref[:, g:g + 1]
        for c in range(tm // CHUNK):
            r0, r1 = c * CHUNK, (c + 1) * CHUNK
            mixed = _dot(wc, v_scr[r0:r1, lo:hi]) + bias
            g_scr[r0:r1, lo:hi] = (gate[r0:r1] * mixed).astype(BF16)

    o_ref[0] = _dot(g_scr[...], wout_ref[...]) + x


def _sgu_layer(h, norm_w, w_in, ln_w, ln_b, w_s, b_s, w_out):
    bsz, seq, _ = h.shape
    tm = SGU_TOKEN_TILE
    const2 = lambda b, i: (0, 0)
    const3 = lambda b, i: (0, 0, 0)
    tile = lambda b, i: (b, i, 0)
    resident = pl.Buffered(1)
    return pl.pallas_call(
        functools.partial(_sgu_kernel, tm=tm),
        grid=(bsz, seq // tm),
        in_specs=[
            pl.BlockSpec((1, tm, D_MODEL), tile),
            pl.BlockSpec((1, D_MODEL), const2),
            pl.BlockSpec((D_MODEL, 3 * D_INNER), const2, pipeline_mode=resident),
            pl.BlockSpec((1, D_INNER), const2),
            pl.BlockSpec((1, D_INNER), const2),
            pl.BlockSpec((N_GROUPS, CHUNK, CHUNK), const3),
            pl.BlockSpec((CHUNK, N_GROUPS), const2),
            pl.BlockSpec((D_INNER, D_MODEL), const2, pipeline_mode=resident),
        ],
        out_specs=pl.BlockSpec((1, tm, D_MODEL), tile),
        out_shape=jax.ShapeDtypeStruct(h.shape, F32),
        scratch_shapes=[
            pltpu.VMEM((tm, D_INNER), BF16),
            pltpu.VMEM((tm, D_INNER), BF16),
        ],
        compiler_params=pltpu.CompilerParams(
            dimension_semantics=("arbitrary", "arbitrary"),
            vmem_limit_bytes=VMEM_LIMIT_BYTES),
        name="sgu_layer",
    )(h, norm_w.reshape(1, D_MODEL), w_in.astype(BF16), ln_w.reshape(1, D_INNER),
      ln_b.reshape(1, D_INNER), w_s, b_s.T, w_out.astype(BF16))


def _ssd_kernel(h_ref, hnext_ref, nw_ref, wzx_ref, wdt_ref, wdtt_ref, cw_ref, cbias_ref, dtb_ref, dtbt_ref,
                alog_ref, alogt_ref, dskip_ref, gnw_ref, wout_ref, fnw_ref,
                o_ref, raw_scr, tail_scr, hn_scr, dtr_scr, dttr_scr, st_scr, y_scr, yprev_scr, xprev_scr,
                *, tm, tiles_per_seq, final_norm):
    step = pl.program_id(0)

    def project_next(x_tile):
        hn_new = _rms_rows(x_tile, nw_ref[...]).astype(BF16)
        hn_scr[...] = hn_new
        raw = _dot(hn_new, wzx_ref[:, D_INNER:])
        for n in range(CONV_DIM // LANES):
            raw_scr[n, CARRY_ROWS:CARRY_ROWS + tm, :] = raw[:, n * LANES:(n + 1) * LANES]
        dtr_scr[...] = _dot(hn_new, wdt_ref[...])
        dttr_scr[...] = _dot_nt(wdtt_ref[...], hn_new)

    @pl.when(step == 0)
    def _():
        project_next(h_ref[0])
        yprev_scr[...] = jnp.zeros_like(yprev_scr)
        xprev_scr[...] = jnp.zeros_like(xprev_scr)

    @pl.when(step % tiles_per_seq == 0)
    def _():
        st_scr[...] = jnp.zeros_like(st_scr)
        raw_scr[:, 0:CARRY_ROWS, :] = jnp.zeros((CONV_DIM // LANES, CARRY_ROWS, LANES), F32)

    out = _dot(yprev_scr[...], wout_ref[...]) + xprev_scr[...]
    if final_norm:
        out = _rms_rows(out, fnw_ref[...])
    o_ref[0] = out

    x_in = h_ref[0]
    hn = hn_scr[...]
    dt = jax.nn.softplus(dtr_scr[...] + dtb_ref[...])
    dtt = jax.nn.softplus(dttr_scr[...] + dtbt_ref[...])
    da = dt * (-jnp.exp(alog_ref[...]))
    dat = dtt * (-jnp.exp(alogt_ref[...]))

    row = lax.broadcasted_iota(jnp.int32, (CHUNK, CHUNK), 0)
    col = lax.broadcasted_iota(jnp.int32, (CHUNK, CHUNK), 1)
    causal = row >= col
    first_head = col < HEADDIM
    lower = causal.astype(BF16)
    upper = (row <= col).astype(BF16)

    n_chunks = tm // CHUNK
    cs, cst, dte = [], [], []
    for c in range(n_chunks):
        r0, r1 = c * CHUNK, (c + 1) * CHUNK
        cs_c = sum(_dot(lower, p) for p in _split3(da[r0:r1]))
        cst_c = sum(_dot(p, upper) for p in _split3(dat[:, r0:r1]))
        last = cs_c[CHUNK - 1:CHUNK, :]
        cs.append(cs_c * LOG2E)
        cst.append((cst_c - jnp.log(dtt[:, r0:r1])) * LOG2E)
        dte.append((last - (cs_c - jnp.log(dt[r0:r1]))) * LOG2E)

    for g in range(N_GROUPS):
        lo, hi = g * GROUP_DIM, (g + 1) * GROUP_DIM
        blo, bhi = D_INNER + g * D_STATE, D_INNER + (g + 1) * D_STATE
        clo, chi = blo + BC_DIM, bhi + BC_DIM

        def conv(c0, c1):
            slabs = []
            for n in range(c0 // LANES, c1 // LANES):
                acc = cbias_ref[:, n * LANES:(n + 1) * LANES]
                for k in range(D_CONV):
                    start = CARRY_ROWS - (D_CONV - 1) + k
                    tap = raw_scr[n, pl.ds(start, tm, stride=1), :]
                    acc = acc + cw_ref[k:k + 1, n * LANES:(n + 1) * LANES] * tap
                slabs.append(_silu(acc))
            return jnp.concatenate(slabs, axis=1)

        xg = conv(lo, hi)
        bg = conv(blo, bhi).astype(BF16)
        cg = conv(clo, chi).astype(BF16)
        zg = _dot(hn, wzx_ref[:, lo:hi])
        xg_bf = xg.astype(BF16)

        for c in range(n_chunks):
            r0, r1 = c * CHUNK, (c + 1) * CHUNK
            b_c, c_c, x_c = bg[r0:r1], cg[r0:r1], xg_bf[r0:r1]
            cb = _dot_nt(c_c, b_c)
            y_heads, ecs_b, dte_b = [], [], []
            for j in range(HEADS_PER_GROUP):
                hd = g * HEADS_PER_GROUP + j
                cs_l = jnp.broadcast_to(cs[c][:, hd:hd + 1], (CHUNK, CHUNK))
                seg = cs_l - cst[c][hd:hd + 1, :]
                decay_dt = jnp.exp2(jnp.where(causal, seg, -jnp.inf))
                w_ls = (cb * decay_dt).astype(BF16)
                y_heads.append(_dot(w_ls, x_c[:, j * HEADDIM:(j + 1) * HEADDIM]))
                ecs_b.append(jnp.exp2(cs_l))
                dte_b.append(jnp.exp2(jnp.broadcast_to(dte[c][:, hd:hd + 1], (CHUNK, CHUNK))))
            y_diag = jnp.concatenate(y_heads, axis=1)

            ecs_x = jnp.concatenate([jnp.where(first_head, ecs_b[j], ecs_b[j + 1])
                                     for j in range(0, HEADS_PER_GROUP, 2)], axis=1)
            dte_x = jnp.concatenate([jnp.where(first_head, dte_b[j], dte_b[j + 1])
                                     for j in range(0, HEADS_PER_GROUP, 2)], axis=1)
            cd_x = ecs_x[CHUNK - 1:CHUNK, :]

            state = st_scr[g]
            y_off = _dot(c_c, state.astype(BF16)) * ecs_x
            xw = (xg[r0:r1] * dte_x).astype(BF16)
            st_scr[g] = state * cd_x + _dot_tn(b_c, xw)

            y = y_diag + y_off + xg[r0:r1] * dskip_ref[:, lo:hi]
            gated = y * _silu(zg[r0:r1])
            y_scr[r0:r1, lo:hi] = _rms_rows(gated, gnw_ref[:, lo:hi]).astype(BF16)

    tail_scr[...] = raw_scr[:, tm:tm + CARRY_ROWS, :]
    project_next(hnext_ref[0])
    raw_scr[:, 0:CARRY_ROWS, :] = tail_scr[...]

    yprev_scr[...] = y_scr[...]
    xprev_scr[...] = x_in


def _ssd_layer(h, norm_w, w_in, conv_w, conv_b, dt_bias, a_log, d_skip, gn_w, w_out, final_norm_w):
    bsz, seq, _ = h.shape
    tm = SSD_TOKEN_TILE
    tiles_per_seq = seq // tm
    n_tiles = bsz * tiles_per_seq
    const2 = lambda s: (0, 0)

    def tile(s):
        t = jnp.minimum(s, n_tiles - 1)
        return (t // tiles_per_seq, t % tiles_per_seq, 0)

    def prev_tile(s):
        t = jnp.maximum(s - 1, 0)
        return (t // tiles_per_seq, t % tiles_per_seq, 0)

    def next_tile(s):
        t = jnp.minimum(s + 1, n_tiles - 1)
        return (t // tiles_per_seq, t % tiles_per_seq, 0)

    resident = pl.Buffered(1)
    w_zx = w_in[:, :D_INNER + CONV_DIM].astype(BF16)
    w_dt = w_in[:, D_INNER + CONV_DIM:].astype(BF16)
    final = final_norm_w is not None
    fnw = final_norm_w if final else jnp.ones((D_MODEL,), F32)
    small = lambda shape: pl.BlockSpec(shape, const2)
    return pl.pallas_call(
        functools.partial(_ssd_kernel, tm=tm, tiles_per_seq=tiles_per_seq, final_norm=final),
        grid=(n_tiles + 1,),
        in_specs=[
            pl.BlockSpec((1, tm, D_MODEL), tile),
            pl.BlockSpec((1, tm, D_MODEL), next_tile),
            small((1, D_MODEL)),
            pl.BlockSpec((D_MODEL, D_INNER + CONV_DIM), const2, pipeline_mode=resident),
            small((D_MODEL, N_HEADS)),
            small((N_HEADS, D_MODEL)),
            small((D_CONV, CONV_DIM)),
            small((1, CONV_DIM)),
            small((1, N_HEADS)),
            small((N_HEADS, 1)),
            small((1, N_HEADS)),
            small((N_HEADS, 1)),
            small((1, D_INNER)),
            small((1, D_INNER)),
            pl.BlockSpec((D_INNER, D_MODEL), const2, pipeline_mode=resident),
            small((1, D_MODEL)),
        ],
        out_specs=pl.BlockSpec((1, tm, D_MODEL), prev_tile),
        out_shape=jax.ShapeDtypeStruct(h.shape, F32),
        scratch_shapes=[
            pltpu.VMEM((CONV_DIM // LANES, CARRY_ROWS + tm, LANES), F32),
            pltpu.VMEM((CONV_DIM // LANES, CARRY_ROWS, LANES), F32),
            pltpu.VMEM((tm, D_MODEL), BF16),
            pltpu.VMEM((tm, N_HEADS), F32),
            pltpu.VMEM((N_HEADS, tm), F32),
            pltpu.VMEM((N_GROUPS, D_STATE, GROUP_DIM), F32),
            pltpu.VMEM((tm, D_INNER), BF16),
            pltpu.VMEM((tm, D_INNER), BF16),
            pltpu.VMEM((tm, D_MODEL), F32),
        ],
        compiler_params=pltpu.CompilerParams(
            dimension_semantics=("arbitrary",),
            vmem_limit_bytes=VMEM_LIMIT_BYTES),
        name="ssd_final" if final else "ssd_layer",
    )(h, h, norm_w.reshape(1, D_MODEL), w_zx, w_dt, w_dt.T, conv_w, conv_b.reshape(1, CONV_DIM),
      dt_bias.reshape(1, N_HEADS), dt_bias.reshape(N_HEADS, 1),
      a_log.reshape(1, N_HEADS), a_log.reshape(N_HEADS, 1),
      jnp.repeat(d_skip, HEADDIM).reshape(1, D_INNER), gn_w.reshape(1, D_INNER),
      w_out.astype(BF16), fnw.reshape(1, D_MODEL))


def kernel(x, norm_w, final_norm_w, a_w_in, a_ln_w, a_ln_b, a_w_s, a_b_s, a_w_out,
           b_w_in, b_conv_w, b_conv_b, b_dt_bias, b_a_log, b_d_skip, b_norm_w, b_w_out):
    depth = norm_w.shape[0]
    h = x
    for i in range(depth):
        k = i // 2
        if i % 2 == 0:
            h = _sgu_layer(h, norm_w[i], a_w_in[k], a_ln_w[k], a_ln_b[k], a_w_s[k], a_b_s[k],
                           a_w_out[k])
        else:
            h = _ssd_layer(h, norm_w[i], b_w_in[k], b_conv_w[k], b_conv_b[k], b_dt_bias[k],
                           b_a_log[k], b_d_skip[k], b_norm_w[k], b_w_out[k],
                           final_norm_w if i == depth - 1 else None)
    return h
```

```python
import functools

import jax
import jax.numpy as jnp
from jax import lax
from jax.experimental import pallas as pl
from jax.experimental.pallas import tpu as pltpu

D_MODEL = 1024
D_INNER = 2048
CHUNK = 128
N_GROUPS = 8
GROUP_DIM = D_INNER // N_GROUPS
HEADDIM = 64
HEADS_PER_GROUP = GROUP_DIM // HEADDIM
N_HEADS = D_INNER // HEADDIM
D_STATE = 128
D_CONV = 4
BC_DIM = N_GROUPS * D_STATE
CONV_DIM = D_INNER + 2 * BC_DIM
NORM_EPS = 1e-6
LN_EPS = 1e-5
CARRY_ROWS = 8

SGU_TOKEN_TILE = 512
SSD_TOKEN_TILE = 256
LANES = 128
LOG2E = 1.4426950408889634
VMEM_LIMIT_BYTES = 56 * 1024 * 1024

BF16 = jnp.bfloat16
F32 = jnp.float32


def _dot(a, b):
    return jnp.dot(a, b, preferred_element_type=F32)


def _dot_nt(a, b):
    return lax.dot_general(a, b, (((1,), (1,)), ((), ())), preferred_element_type=F32)


def _dot_tn(a, b):
    return lax.dot_general(a, b, (((0,), (0,)), ((), ())), preferred_element_type=F32)


def _silu(x):
    half = 0.5 * x
    return half + half * jnp.tanh(half)


def _rms_rows(x, w):
    return x * lax.rsqrt(jnp.mean(x * x, axis=-1, keepdims=True) + NORM_EPS) * w


def _split3(a):
    hi = a.astype(BF16)
    r1 = a - hi.astype(F32)
    mid = r1.astype(BF16)
    lo = (r1 - mid.astype(F32)).astype(BF16)
    return hi, mid, lo


def _sgu_kernel(h_ref, nw_ref, win_ref, lnw_ref, lnb_ref, ws_ref, bst_ref, wout_ref,
                o_ref, v_scr, g_scr, *, tm):
    x = h_ref[0]
    hn = _rms_rows(x, nw_ref[...]).astype(BF16)

    v = _dot(hn, win_ref[:, 2 * D_INNER:3 * D_INNER])
    mu = jnp.mean(v, axis=-1, keepdims=True)
    vc = v - mu
    var = jnp.mean(vc * vc, axis=-1, keepdims=True)
    vn = vc * lax.rsqrt(var + LN_EPS) * lnw_ref[...] + lnb_ref[...]
    v_scr[...] = vn.astype(BF16)

    row = lax.broadcasted_iota(jnp.int32, (CHUNK, CHUNK), 0)
    col = lax.broadcasted_iota(jnp.int32, (CHUNK, CHUNK), 1)
    causal = row >= col
    for g in range(N_GROUPS):
        lo, hi = g * GROUP_DIM, (g + 1) * GROUP_DIM
        z = _dot(hn, win_ref[:, lo:hi])
        u = _dot(hn, win_ref[:, D_INNER + lo:D_INNER + hi])
        gate = u * _silu(z)
        wc = jnp.where(causal, ws_ref[g], 0.0).astype(BF16)
        bias = bst_ref[:, g:g + 1]
        for c in range(tm // CHUNK):
            r0, r1 = c * CHUNK, (c + 1) * CHUNK
            mixed = _dot(wc, v_scr[r0:r1, lo:hi]) + bias
            g_scr[r0:r1, lo:hi] = (gate[r0:r1] * mixed).astype(BF16)

    o_ref[0] = _dot(g_scr[...], wout_ref[...]) + x


def _sgu_layer(h, norm_w, w_in_all, k, ln_w, ln_b, w_s, b_s, w_out_all):
    bsz, seq, _ = h.shape
    tm = SGU_TOKEN_TILE
    const2 = lambda b, i: (0, 0)
    const3 = lambda b, i: (0, 0, 0)
    layer_k = lambda b, i: (k, 0, 0)
    tile = lambda b, i: (b, i, 0)
    resident = pl.Buffered(1)
    return pl.pallas_call(
        functools.partial(_sgu_kernel, tm=tm),
        grid=(bsz, seq // tm),
        in_specs=[
            pl.BlockSpec((1, tm, D_MODEL), tile),
            pl.BlockSpec((1, D_MODEL), const2),
            pl.BlockSpec((None, D_MODEL, 3 * D_INNER), layer_k, pipeline_mode=resident),
            pl.BlockSpec((1, D_INNER), const2),
            pl.BlockSpec((1, D_INNER), const2),
            pl.BlockSpec((N_GROUPS, CHUNK, CHUNK), const3),
            pl.BlockSpec((CHUNK, N_GROUPS), const2),
            pl.BlockSpec((None, D_INNER, D_MODEL), layer_k, pipeline_mode=resident),
        ],
        out_specs=pl.BlockSpec((1, tm, D_MODEL), tile),
        out_shape=jax.ShapeDtypeStruct(h.shape, F32),
        scratch_shapes=[
            pltpu.VMEM((tm, D_INNER), BF16),
            pltpu.VMEM((tm, D_INNER), BF16),
        ],
        compiler_params=pltpu.CompilerParams(
            dimension_semantics=("arbitrary", "arbitrary"),
            vmem_limit_bytes=VMEM_LIMIT_BYTES),
        name="sgu_layer",
    )(h, norm_w.reshape(1, D_MODEL), w_in_all, ln_w.reshape(1, D_INNER),
      ln_b.reshape(1, D_INNER), w_s, b_s.T, w_out_all)


def _ssd_kernel(h_ref, hnext_ref, nw_ref, wzx_ref, wdt_ref, wdtt_ref, cw_ref, cbias_ref, dtb_ref, dtbt_ref,
                alog_ref, alogt_ref, dskip_ref, gnw_ref, wout_ref, fnw_ref,
                o_ref, raw_scr, tail_scr, hn_scr, dtr_scr, dttr_scr, st_scr, y_scr, yprev_scr, xprev_scr,
                *, tm, tiles_per_seq, final_norm):
    step = pl.program_id(0)

    def project_next(x_tile):
        hn_new = _rms_rows(x_tile, nw_ref[...]).astype(BF16)
        hn_scr[...] = hn_new
        raw = _dot(hn_new, wzx_ref[:, D_INNER:])
        for n in range(CONV_DIM // LANES):
            raw_scr[n, CARRY_ROWS:CARRY_ROWS + tm, :] = raw[:, n * LANES:(n + 1) * LANES]
        dtr_scr[...] = _dot(hn_new, wdt_ref[...])
        dttr_scr[...] = _dot_nt(wdtt_ref[...], hn_new)

    @pl.when(step == 0)
    def _():
        project_next(h_ref[0])
        yprev_scr[...] = jnp.zeros_like(yprev_scr)
        xprev_scr[...] = jnp.zeros_like(xprev_scr)

    @pl.when(step % tiles_per_seq == 0)
    def _():
        st_scr[...] = jnp.zeros_like(st_scr)
        raw_scr[:, 0:CARRY_ROWS, :] = jnp.zeros((CONV_DIM // LANES, CARRY_ROWS, LANES), F32)

    out = _dot(yprev_scr[...], wout_ref[...]) + xprev_scr[...]
    if final_norm:
        out = _rms_rows(out, fnw_ref[...])
    o_ref[0] = out

    x_in = h_ref[0]
    hn = hn_scr[...]
    dt = jax.nn.softplus(dtr_scr[...] + dtb_ref[...])
    dtt = jax.nn.softplus(dttr_scr[...] + dtbt_ref[...])
    da = dt * (-jnp.exp(alog_ref[...]))
    dat = dtt * (-jnp.exp(alogt_ref[...]))

    row = lax.broadcasted_iota(jnp.int32, (CHUNK, CHUNK), 0)
    col = lax.broadcasted_iota(jnp.int32, (CHUNK, CHUNK), 1)
    causal = row >= col
    first_head = col < HEADDIM
    lower = causal.astype(BF16)
    upper = (row <= col).astype(BF16)

    n_chunks = tm // CHUNK
    cs, cst, dte = [], [], []
    for c in range(n_chunks):
        r0, r1 = c * CHUNK, (c + 1) * CHUNK
        cs_c = sum(_dot(lower, p) for p in _split3(da[r0:r1]))
        cst_c = sum(_dot(p, upper) for p in _split3(dat[:, r0:r1]))
        last = cs_c[CHUNK - 1:CHUNK, :]
        cs.append(cs_c * LOG2E)
        cst.append((cst_c - jnp.log(dtt[:, r0:r1])) * LOG2E)
        dte.append((last - (cs_c - jnp.log(dt[r0:r1]))) * LOG2E)

    for g in range(N_GROUPS):
        lo, hi = g * GROUP_DIM, (g + 1) * GROUP_DIM
        blo, bhi = D_INNER + g * D_STATE, D_INNER + (g + 1) * D_STATE
        clo, chi = blo + BC_DIM, bhi + BC_DIM

        def conv(c0, c1):
            slabs = []
            for n in range(c0 // LANES, c1 // LANES):
                acc = cbias_ref[:, n * LANES:(n + 1) * LANES]
                for k in range(D_CONV):
                    start = CARRY_ROWS - (D_CONV - 1) + k
                    tap = raw_scr[n, pl.ds(start, tm, stride=1), :]
                    acc = acc + cw_ref[k:k + 1, n * LANES:(n + 1) * LANES] * tap
                slabs.append(_silu(acc))
            return jnp.concatenate(slabs, axis=1)

        xg = conv(lo, hi)
        bg = conv(blo, bhi).astype(BF16)
        cg = conv(clo, chi).astype(BF16)
        zg = _dot(hn, wzx_ref[:, lo:hi])
        xg_bf = xg.astype(BF16)

        for c in range(n_chunks):
            r0, r1 = c * CHUNK, (c + 1) * CHUNK
            b_c, c_c, x_c = bg[r0:r1], cg[r0:r1], xg_bf[r0:r1]
            cb = _dot_nt(c_c, b_c)
            y_heads, ecs_b, dte_b = [], [], []
            for j in range(HEADS_PER_GROUP):
                hd = g * HEADS_PER_GROUP + j
                cs_l = jnp.broadcast_to(cs[c][:, hd:hd + 1], (CHUNK, CHUNK))
                seg = cs_l - cst[c][hd:hd + 1, :]
                decay_dt = jnp.exp2(jnp.where(causal, seg, -jnp.inf))
                w_ls = (cb * decay_dt).astype(BF16)
                y_heads.append(_dot(w_ls, x_c[:, j * HEADDIM:(j + 1) * HEADDIM]))
                ecs_b.append(jnp.exp2(cs_l))
                dte_b.append(jnp.exp2(jnp.broadcast_to(dte[c][:, hd:hd + 1], (CHUNK, CHUNK))))
            y_diag = jnp.concatenate(y_heads, axis=1)

            ecs_x = jnp.concatenate([jnp.where(first_head, ecs_b[j], ecs_b[j + 1])
                                     for j in range(0, HEADS_PER_GROUP, 2)], axis=1)
            dte_x = jnp.concatenate([jnp.where(first_head, dte_b[j], dte_b[j + 1])
                                     for j in range(0, HEADS_PER_GROUP, 2)], axis=1)
            cd_x = ecs_x[CHUNK - 1:CHUNK, :]

            state = st_scr[g]
            y_off = _dot(c_c, state.astype(BF16)) * ecs_x
            xw = (xg[r0:r1] * dte_x).astype(BF16)
            st_scr[g] = state * cd_x + _dot_tn(b_c, xw)

            y = y_diag + y_off + xg[r0:r1] * dskip_ref[:, lo:hi]
            gated = y * _silu(zg[r0:r1])
            y_scr[r0:r1, lo:hi] = _rms_rows(gated, gnw_ref[:, lo:hi]).astype(BF16)

    tail_scr[...] = raw_scr[:, tm:tm + CARRY_ROWS, :]
    project_next(hnext_ref[0])
    raw_scr[:, 0:CARRY_ROWS, :] = tail_scr[...]

    yprev_scr[...] = y_scr[...]
    xprev_scr[...] = x_in


def _ssd_layer(h, norm_w, w_in_all, k, conv_w, conv_b, dt_bias, a_log, d_skip, gn_w, w_out_all,
               final_norm_w):
    bsz, seq, _ = h.shape
    tm = SSD_TOKEN_TILE
    tiles_per_seq = seq // tm
    n_tiles = bsz * tiles_per_seq
    const2 = lambda s: (0, 0)

    def tile(s):
        t = jnp.minimum(s, n_tiles - 1)
        return (t // tiles_per_seq, t % tiles_per_seq, 0)

    def prev_tile(s):
        t = jnp.maximum(s - 1, 0)
        return (t // tiles_per_seq, t % tiles_per_seq, 0)

    def next_tile(s):
        t = jnp.minimum(s + 1, n_tiles - 1)
        return (t // tiles_per_seq, t % tiles_per_seq, 0)

    resident = pl.Buffered(1)
    layer_k = lambda s: (k, 0, 0)
    w_dt = w_in_all[k, :, D_INNER + CONV_DIM:]
    final = final_norm_w is not None
    fnw = final_norm_w if final else jnp.ones((D_MODEL,), F32)
    small = lambda shape: pl.BlockSpec(shape, const2)
    return pl.pallas_call(
        functools.partial(_ssd_kernel, tm=tm, tiles_per_seq=tiles_per_seq, final_norm=final),
        grid=(n_tiles + 1,),
        in_specs=[
            pl.BlockSpec((1, tm, D_MODEL), tile),
            pl.BlockSpec((1, tm, D_MODEL), next_tile),
            small((1, D_MODEL)),
            pl.BlockSpec((None, D_MODEL, D_INNER + CONV_DIM), layer_k, pipeline_mode=resident),
            small((D_MODEL, N_HEADS)),
            small((N_HEADS, D_MODEL)),
            small((D_CONV, CONV_DIM)),
            small((1, CONV_DIM)),
            small((1, N_HEADS)),
            small((N_HEADS, 1)),
            small((1, N_HEADS)),
            small((N_HEADS, 1)),
            small((1, D_INNER)),
            small((1, D_INNER)),
            pl.BlockSpec((None, D_INNER, D_MODEL), layer_k, pipeline_mode=resident),
            small((1, D_MODEL)),
        ],
        out_specs=pl.BlockSpec((1, tm, D_MODEL), prev_tile),
        out_shape=jax.ShapeDtypeStruct(h.shape, F32),
        scratch_shapes=[
            pltpu.VMEM((CONV_DIM // LANES, CARRY_ROWS + tm, LANES), F32),
            pltpu.VMEM((CONV_DIM // LANES, CARRY_ROWS, LANES), F32),
            pltpu.VMEM((tm, D_MODEL), BF16),
            pltpu.VMEM((tm, N_HEADS), F32),
            pltpu.VMEM((N_HEADS, tm), F32),
            pltpu.VMEM((N_GROUPS, D_STATE, GROUP_DIM), F32),
            pltpu.VMEM((tm, D_INNER), BF16),
            pltpu.VMEM((tm, D_INNER), BF16),
            pltpu.VMEM((tm, D_MODEL), F32),
        ],
        compiler_params=pltpu.CompilerParams(
            dimension_semantics=("arbitrary",),
            vmem_limit_bytes=VMEM_LIMIT_BYTES),
        name="ssd_final" if final else "ssd_layer",
    )(h, h, norm_w.reshape(1, D_MODEL), w_in_all, w_dt, w_dt.T, conv_w, conv_b.reshape(1, CONV_DIM),
      dt_bias.reshape(1, N_HEADS), dt_bias.reshape(N_HEADS, 1),
      a_log.reshape(1, N_HEADS), a_log.reshape(N_HEADS, 1),
      jnp.repeat(d_skip, HEADDIM).reshape(1, D_INNER), gn_w.reshape(1, D_INNER),
      w_out_all, fnw.reshape(1, D_MODEL))


def kernel(x, norm_w, final_norm_w, a_w_in, a_ln_w, a_ln_b, a_w_s, a_b_s, a_w_out,
           b_w_in, b_conv_w, b_conv_b, b_dt_bias, b_a_log, b_d_skip, b_norm_w, b_w_out):
    depth = norm_w.shape[0]
    a_w_in, a_w_out, b_w_in, b_w_out = (w.astype(BF16) for w in (a_w_in, a_w_out, b_w_in, b_w_out))
    h = x
    for i in range(depth):
        k = i // 2
        if i % 2 == 0:
            h = _sgu_layer(h, norm_w[i], a_w_in, k, a_ln_w[k], a_ln_b[k], a_w_s[k], a_b_s[k], a_w_out)
        else:
            h = _ssd_layer(h, norm_w[i], b_w_in, k, b_conv_w[k], b_conv_b[k], b_dt_bias[k],
                           b_a_log[k], b_d_skip[k], b_norm_w[k], b_w_out,
                           final_norm_w if i == depth - 1 else None)
    return h
```

```python
import functools

import jax
import jax.numpy as jnp
from jax import lax
from jax.experimental import pallas as pl
from jax.experimental.pallas import tpu as pltpu

D_MODEL = 1024
D_INNER = 2048
CHUNK = 128
N_GROUPS = 8
GROUP_DIM = D_INNER // N_GROUPS
HEADDIM = 64
HEADS_PER_GROUP = GROUP_DIM // HEADDIM
N_HEADS = D_INNER // HEADDIM
D_STATE = 128
D_CONV = 4
BC_DIM = N_GROUPS * D_STATE
CONV_DIM = D_INNER + 2 * BC_DIM
NORM_EPS = 1e-6
LN_EPS = 1e-5
CARRY_ROWS = 8

SGU_TOKEN_TILE = 512
SSD_TOKEN_TILE = 256
LANES = 128
LOG2E = 1.4426950408889634
VMEM_LIMIT_BYTES = 56 * 1024 * 1024

BF16 = jnp.bfloat16
F32 = jnp.float32


def _dot(a, b):
    return jnp.dot(a, b, preferred_element_type=F32)


def _dot_nt(a, b):
    return lax.dot_general(a, b, (((1,), (1,)), ((), ())), preferred_element_type=F32)


def _dot_tn(a, b):
    return lax.dot_general(a, b, (((0,), (0,)), ((), ())), preferred_element_type=F32)


def _silu(x):
    half = 0.5 * x
    return half + half * jnp.tanh(half)


def _rms_rows(x, w):
    return x * lax.rsqrt(jnp.mean(x * x, axis=-1, keepdims=True) + NORM_EPS) * w


def _split3(a):
    hi = a.astype(BF16)
    r1 = a - hi.astype(F32)
    mid = r1.astype(BF16)
    lo = (r1 - mid.astype(F32)).astype(BF16)
    return hi, mid, lo


def _sgu_kernel(h_ref, nw_ref, win_ref, lnw_ref, lnb_ref, ws_ref, bst_ref, wout_ref,
                o_ref, v_scr, g_scr, *, tm):
    x = h_ref[0]
    hn = _rms_rows(x, nw_ref[...]).astype(BF16)

    v = _dot(hn, win_ref[:, 2 * D_INNER:3 * D_INNER])
    mu = jnp.mean(v, axis=-1, keepdims=True)
    vc = v - mu
    var = jnp.mean(vc * vc, axis=-1, keepdims=True)
    vn = vc * lax.rsqrt(var + LN_EPS) * lnw_ref[...] + lnb_ref[...]
    v_scr[...] = vn.astype(BF16)

    row = lax.broadcasted_iota(jnp.int32, (CHUNK, CHUNK), 0)
    col = lax.broadcasted_iota(jnp.int32, (CHUNK, CHUNK), 1)
    causal = row >= col
    for g in range(N_GROUPS):
        lo, hi = g * GROUP_DIM, (g + 1) * GROUP_DIM
        z = _dot(hn, win_ref[:, lo:hi])
        u = _dot(hn, win_ref[:, D_INNER + lo:D_INNER + hi])
        gate = u * _silu(z)
        wc = jnp.where(causal, ws_ref[g], 0.0).astype(BF16)
        bias = bst_ref[:, g:g + 1]
        for c in range(tm // CHUNK):
            r0, r1 = c * CHUNK, (c + 1) * CHUNK
            mixed = _dot(wc, v_scr[r0:r1, lo:hi]) + bias
            g_scr[r0:r1, lo:hi] = (gate[r0:r1] * mixed).astype(BF16)

    o_ref[0] = _dot(g_scr[...], wout_ref[...]) + x


def _sgu_layer(h, norm_w, w_in_all, k, ln_w, ln_b, w_s, b_s, w_out_all):
    bsz, seq, _ = h.shape
    tm = SGU_TOKEN_TILE
    const2 = lambda b, i: (0, 0)
    const3 = lambda b, i: (0, 0, 0)
    layer_k = lambda b, i: (k, 0, 0)
    tile = lambda b, i: (b, i, 0)
    resident = pl.Buffered(1)
    return pl.pallas_call(
        functools.partial(_sgu_kernel, tm=tm),
        grid=(bsz, seq // tm),
        in_specs=[
            pl.BlockSpec((1, tm, D_MODEL), tile),
            pl.BlockSpec((1, D_MODEL), const2),
            pl.BlockSpec((None, D_MODEL, 3 * D_INNER), layer_k, pipeline_mode=resident),
            pl.BlockSpec((1, D_INNER), const2),
            pl.BlockSpec((1, D_INNER), const2),
            pl.BlockSpec((N_GROUPS, CHUNK, CHUNK), const3),
            pl.BlockSpec((CHUNK, N_GROUPS), const2),
            pl.BlockSpec((None, D_INNER, D_MODEL), layer_k, pipeline_mode=resident),
        ],
        out_specs=pl.BlockSpec((1, tm, D_MODEL), tile),
        out_shape=jax.ShapeDtypeStruct(h.shape, F32),
        scratch_shapes=[
            pltpu.VMEM((tm, D_INNER), BF16),
            pltpu.VMEM((tm, D_INNER), BF16),
        ],
        compiler_params=pltpu.CompilerParams(
            dimension_semantics=("arbitrary", "arbitrary"),
            vmem_limit_bytes=VMEM_LIMIT_BYTES),
        name="sgu_layer",
    )(h, norm_w.reshape(1, D_MODEL), w_in_all, ln_w.reshape(1, D_INNER),
      ln_b.reshape(1, D_INNER), w_s, b_s.T, w_out_all)


def _ssd_kernel(h_ref, hnext_ref, nw_ref, wzx_ref, wdt_ref, wdtt_ref, cw_ref, cbias_ref, dtb_ref, dtbt_ref,
                alog_ref, alogt_ref, dskip_ref, gnw_ref, wout_ref, fnw_ref,
                o_ref, raw_scr, tail_scr, hn_scr, dtr_scr, dttr_scr, st_scr, y_scr, yprev_scr, xprev_scr,
                *, tm, tiles_per_seq, final_norm):
    step = pl.program_id(0)

    def project_next(x_tile):
        hn_new = _rms_rows(x_tile, nw_ref[...]).astype(BF16)
        hn_scr[...] = hn_new
        raw = _dot(hn_new, wzx_ref[:, D_INNER:])
        for n in range(CONV_DIM // LANES):
            raw_scr[n, CARRY_ROWS:CARRY_ROWS + tm, :] = raw[:, n * LANES:(n + 1) * LANES]
        dtr_scr[...] = _dot(hn_new, wdt_ref[...])
        dttr_scr[...] = _dot_nt(wdtt_ref[...], hn_new)

    @pl.when(step == 0)
    def _():
        project_next(h_ref[0])
        yprev_scr[...] = jnp.zeros_like(yprev_scr)
        xprev_scr[...] = jnp.zeros_like(xprev_scr)

    @pl.when(step % tiles_per_seq == 0)
    def _():
        st_scr[...] = jnp.zeros_like(st_scr)
        raw_scr[:, 0:CARRY_ROWS, :] = jnp.zeros((CONV_DIM // LANES, CARRY_ROWS, LANES), F32)

    out = _dot(yprev_scr[...], wout_ref[...]) + xprev_scr[...]
    if final_norm:
        out = _rms_rows(out, fnw_ref[...])
    o_ref[0] = out

    x_in = h_ref[0]
    hn = hn_scr[...]
    dt = jax.nn.softplus(dtr_scr[...] + dtb_ref[...])
    dtt = jax.nn.softplus(dttr_scr[...] + dtbt_ref[...])
    da = dt * (-jnp.exp(alog_ref[...]))
    dat = dtt * (-jnp.exp(alogt_ref[...]))

    row = lax.broadcasted_iota(jnp.int32, (CHUNK, CHUNK), 0)
    col = lax.broadcasted_iota(jnp.int32, (CHUNK, CHUNK), 1)
    causal = row >= col
    first_head = col < HEADDIM
    lower = causal.astype(BF16)
    upper = (row <= col).astype(BF16)

    n_chunks = tm // CHUNK
    cs, cst, dte = [], [], []
    for c in range(n_chunks):
        r0, r1 = c * CHUNK, (c + 1) * CHUNK
        cs_c = sum(_dot(lower, p) for p in _split3(da[r0:r1]))
        cst_c = sum(_dot(p, upper) for p in _split3(dat[:, r0:r1]))
        last = cs_c[CHUNK - 1:CHUNK, :]
        cs.append(cs_c * LOG2E)
        cst.append((cst_c - jnp.log(dtt[:, r0:r1])) * LOG2E)
        dte.append((last - (cs_c - jnp.log(dt[r0:r1]))) * LOG2E)

    for g in range(N_GROUPS):
        lo, hi = g * GROUP_DIM, (g + 1) * GROUP_DIM
        blo, bhi = D_INNER + g * D_STATE, D_INNER + (g + 1) * D_STATE
        clo, chi = blo + BC_DIM, bhi + BC_DIM

        def conv(c0, c1):
            slabs = []
            for n in range(c0 // LANES, c1 // LANES):
                acc = cbias_ref[:, n * LANES:(n + 1) * LANES]
                for k in range(D_CONV):
                    start = CARRY_ROWS - (D_CONV - 1) + k
                    tap = raw_scr[n, pl.ds(start, tm, stride=1), :]
                    acc = acc + cw_ref[k:k + 1, n * LANES:(n + 1) * LANES] * tap
                slabs.append(_silu(acc))
            return jnp.concatenate(slabs, axis=1)

        xg = conv(lo, hi)
        bg = conv(blo, bhi).astype(BF16)
        cg = conv(clo, chi).astype(BF16)
        zg = _dot(hn, wzx_ref[:, lo:hi])
        xg_bf = xg.astype(BF16)

        for c in range(n_chunks):
            r0, r1 = c * CHUNK, (c + 1) * CHUNK
            b_c, c_c, x_c = bg[r0:r1], cg[r0:r1], xg_bf[r0:r1]
            cb = _dot_nt(c_c, b_c)
            y_heads, ecs_b, dte_b = [], [], []
            for j in range(HEADS_PER_GROUP):
                hd = g * HEADS_PER_GROUP + j
                cs_l = jnp.broadcast_to(cs[c][:, hd:hd + 1], (CHUNK, CHUNK))
                seg = cs_l - cst[c][hd:hd + 1, :]
                decay_dt = jnp.exp2(jnp.where(causal, seg, -jnp.inf))
                w_ls = (cb * decay_dt).astype(BF16)
                y_heads.append(_dot(w_ls, x_c[:, j * HEADDIM:(j + 1) * HEADDIM]))
                ecs_b.append(jnp.exp2(cs_l))
                dte_b.append(jnp.exp2(jnp.broadcast_to(dte[c][:, hd:hd + 1], (CHUNK, CHUNK))))
            y_diag = jnp.concatenate(y_heads, axis=1)

            ecs_x = jnp.concatenate([jnp.where(first_head, ecs_b[j], ecs_b[j + 1])
                                     for j in range(0, HEADS_PER_GROUP, 2)], axis=1)
            dte_x = jnp.concatenate([jnp.where(first_head, dte_b[j], dte_b[j + 1])
                                     for j in range(0, HEADS_PER_GROUP, 2)], axis=1)
            cd_x = ecs_x[CHUNK - 1:CHUNK, :]

            state = st_scr[g]
            y_off = _dot(c_c, state.astype(BF16)) * ecs_x
            xw = (xg[r0:r1] * dte_x).astype(BF16)
            st_scr[g] = state * cd_x + _dot_tn(b_c, xw)

            y = y_diag + y_off + xg[r0:r1] * dskip_ref[:, lo:hi]
            gated = y * _silu(zg[r0:r1])
            y_scr[r0:r1, lo:hi] = _rms_rows(gated, gnw_ref[:, lo:hi]).astype(BF16)

    tail_scr[...] = raw_scr[:, tm:tm + CARRY_ROWS, :]
    project_next(hnext_ref[0])
    raw_scr[:, 0:CARRY_ROWS, :] = tail_scr[...]

    yprev_scr[...] = y_scr[...]
    xprev_scr[...] = x_in


def _ssd_layer(h, norm_w, w_zx_all, w_dt, k, conv_w, conv_b, dt_bias, a_log, d_skip, gn_w, w_out_all,
               final_norm_w):
    bsz, seq, _ = h.shape
    tm = SSD_TOKEN_TILE
    tiles_per_seq = seq // tm
    n_tiles = bsz * tiles_per_seq
    const2 = lambda s: (0, 0)

    def tile(s):
        t = jnp.minimum(s, n_tiles - 1)
        return (t // tiles_per_seq, t % tiles_per_seq, 0)

    def prev_tile(s):
        t = jnp.maximum(s - 1, 0)
        return (t // tiles_per_seq, t % tiles_per_seq, 0)

    def next_tile(s):
        t = jnp.minimum(s + 1, n_tiles - 1)
        return (t // tiles_per_seq, t % tiles_per_seq, 0)

    resident = pl.Buffered(1)
    layer_k = lambda s: (k, 0, 0)
    final = final_norm_w is not None
    fnw = final_norm_w if final else jnp.ones((D_MODEL,), F32)
    small = lambda shape: pl.BlockSpec(shape, const2)
    return pl.pallas_call(
        functools.partial(_ssd_kernel, tm=tm, tiles_per_seq=tiles_per_seq, final_norm=final),
        grid=(n_tiles + 1,),
        in_specs=[
            pl.BlockSpec((1, tm, D_MODEL), tile),
            pl.BlockSpec((1, tm, D_MODEL), next_tile),
            small((1, D_MODEL)),
            pl.BlockSpec((None, D_MODEL, D_INNER + CONV_DIM), layer_k, pipeline_mode=resident),
            small((D_MODEL, N_HEADS)),
            small((N_HEADS, D_MODEL)),
            small((D_CONV, CONV_DIM)),
            small((1, CONV_DIM)),
            small((1, N_HEADS)),
            small((N_HEADS, 1)),
            small((1, N_HEADS)),
            small((N_HEADS, 1)),
            small((1, D_INNER)),
            small((1, D_INNER)),
            pl.BlockSpec((None, D_INNER, D_MODEL), layer_k, pipeline_mode=resident),
            small((1, D_MODEL)),
        ],
        out_specs=pl.BlockSpec((1, tm, D_MODEL), prev_tile),
        out_shape=jax.ShapeDtypeStruct(h.shape, F32),
        scratch_shapes=[
            pltpu.VMEM((CONV_DIM // LANES, CARRY_ROWS + tm, LANES), F32),
            pltpu.VMEM((CONV_DIM // LANES, CARRY_ROWS, LANES), F32),
            pltpu.VMEM((tm, D_MODEL), BF16),
            pltpu.VMEM((tm, N_HEADS), F32),
            pltpu.VMEM((N_HEADS, tm), F32),
            pltpu.VMEM((N_GROUPS, D_STATE, GROUP_DIM), F32),
            pltpu.VMEM((tm, D_INNER), BF16),
            pltpu.VMEM((tm, D_INNER), BF16),
            pltpu.VMEM((tm, D_MODEL), F32),
        ],
        compiler_params=pltpu.CompilerParams(
            dimension_semantics=("arbitrary",),
            vmem_limit_bytes=VMEM_LIMIT_BYTES),
        name="ssd_final" if final else "ssd_layer",
    )(h, h, norm_w.reshape(1, D_MODEL), w_zx_all, w_dt, w_dt.T, conv_w, conv_b.reshape(1, CONV_DIM),
      dt_bias.reshape(1, N_HEADS), dt_bias.reshape(N_HEADS, 1),
      a_log.reshape(1, N_HEADS), a_log.reshape(N_HEADS, 1),
      jnp.repeat(d_skip, HEADDIM).reshape(1, D_INNER), gn_w.reshape(1, D_INNER),
      w_out_all, fnw.reshape(1, D_MODEL))


def kernel(x, norm_w, final_norm_w, a_w_in, a_ln_w, a_ln_b, a_w_s, a_b_s, a_w_out,
           b_w_in, b_conv_w, b_conv_b, b_dt_bias, b_a_log, b_d_skip, b_norm_w, b_w_out):
    depth = norm_w.shape[0]
    a_w_in, a_w_out, b_w_out = (w.astype(BF16) for w in (a_w_in, a_w_out, b_w_out))
    b_w_zx = b_w_in[:, :, :D_INNER + CONV_DIM].astype(BF16)
    b_w_dt = b_w_in[:, :, D_INNER + CONV_DIM:].astype(BF16)
    h = x
    for i in range(depth):
        k = i // 2
        if i % 2 == 0:
            h = _sgu_layer(h, norm_w[i], a_w_in, k, a_ln_w[k], a_ln_b[k], a_w_s[k], a_b_s[k], a_w_out)
        else:
            h = _ssd_layer(h, norm_w[i], b_w_zx, b_w_dt[k], k, b_conv_w[k], b_conv_b[k], b_dt_bias[k],
                           b_a_log[k], b_d_skip[k], b_norm_w[k], b_w_out,
                           final_norm_w if i == depth - 1 else None)
    return h
```

```python
import functools

import jax
import jax.numpy as jnp
from jax import lax
from jax.experimental import pallas as pl
from jax.experimental.pallas import tpu as pltpu

D_MODEL = 1024
D_INNER = 2048
CHUNK = 128
N_GROUPS = 8
GROUP_DIM = D_INNER // N_GROUPS
HEADDIM = 64
HEADS_PER_GROUP = GROUP_DIM // HEADDIM
N_HEADS = D_INNER // HEADDIM
D_STATE = 128
D_CONV = 4
BC_DIM = N_GROUPS * D_STATE
CONV_DIM = D_INNER + 2 * BC_DIM
NORM_EPS = 1e-6
LN_EPS = 1e-5
CARRY_ROWS = 8

SGU_TOKEN_TILE = 512
SSD_TOKEN_TILE = 256
LANES = 128
GROUP_RAW = GROUP_DIM + 2 * D_STATE
LOG2E = 1.4426950408889634
VMEM_LIMIT_BYTES = 56 * 1024 * 1024

BF16 = jnp.bfloat16
F32 = jnp.float32


def _dot(a, b):
    return jnp.dot(a, b, preferred_element_type=F32)


def _dot_nt(a, b):
    return lax.dot_general(a, b, (((1,), (1,)), ((), ())), preferred_element_type=F32)


def _dot_tn(a, b):
    return lax.dot_general(a, b, (((0,), (0,)), ((), ())), preferred_element_type=F32)


def _silu(x):
    half = 0.5 * x
    return half + half * jnp.tanh(half)


def _rms_rows(x, w):
    return x * lax.rsqrt(jnp.mean(x * x, axis=-1, keepdims=True) + NORM_EPS) * w


def _split3(a):
    hi = a.astype(BF16)
    r1 = a - hi.astype(F32)
    mid = r1.astype(BF16)
    lo = (r1 - mid.astype(F32)).astype(BF16)
    return hi, mid, lo


def _sgu_kernel(h_ref, nw_ref, win_ref, lnw_ref, lnb_ref, ws_ref, bst_ref, wout_ref,
                o_ref, v_scr, g_scr, *, tm):
    x = h_ref[0]
    hn = _rms_rows(x, nw_ref[...]).astype(BF16)

    v = _dot(hn, win_ref[:, 2 * D_INNER:3 * D_INNER])
    mu = jnp.mean(v, axis=-1, keepdims=True)
    vc = v - mu
    var = jnp.mean(vc * vc, axis=-1, keepdims=True)
    vn = vc * lax.rsqrt(var + LN_EPS) * lnw_ref[...] + lnb_ref[...]
    v_scr[...] = vn.astype(BF16)

    row = lax.broadcasted_iota(jnp.int32, (CHUNK, CHUNK), 0)
    col = lax.broadcasted_iota(jnp.int32, (CHUNK, CHUNK), 1)
    causal = row >= col
    for g in range(N_GROUPS):
        lo, hi = g * GROUP_DIM, (g + 1) * GROUP_DIM
        z = _dot(hn, win_ref[:, lo:hi])
        u = _dot(hn, win_ref[:, D_INNER + lo:D_INNER + hi])
        gate = u * _silu(z)
        wc = jnp.where(causal, ws_ref[g], 0.0).astype(BF16)
        bias = bst_ref[:, g:g + 1]
        for c in range(tm // CHUNK):
            r0, r1 = c * CHUNK, (c + 1) * CHUNK
            mixed = _dot(wc, v_scr[r0:r1, lo:hi]) + bias
            g_scr[r0:r1, lo:hi] = (gate[r0:r1] * mixed).astype(BF16)

    o_ref[0] = _dot(g_scr[...], wout_ref[...]) + x


def _sgu_layer(h, norm_w, w_in_all, k, ln_w, ln_b, w_s, b_s, w_out_all):
    bsz, seq, _ = h.shape
    tm = SGU_TOKEN_TILE
    const2 = lambda b, i: (0, 0)
    const3 = lambda b, i: (0, 0, 0)
    layer_k = lambda b, i: (k, 0, 0)
    tile = lambda b, i: (b, i, 0)
    resident = pl.Buffered(1)
    return pl.pallas_call(
        functools.partial(_sgu_kernel, tm=tm),
        grid=(bsz, seq // tm),
        in_specs=[
            pl.BlockSpec((1, tm, D_MODEL), tile),
            pl.BlockSpec((1, D_MODEL), const2),
            pl.BlockSpec((None, D_MODEL, 3 * D_INNER), layer_k, pipeline_mode=resident),
            pl.BlockSpec((1, D_INNER), const2),
            pl.BlockSpec((1, D_INNER), const2),
            pl.BlockSpec((N_GROUPS, CHUNK, CHUNK), const3),
            pl.BlockSpec((CHUNK, N_GROUPS), const2),
            pl.BlockSpec((None, D_INNER, D_MODEL), layer_k, pipeline_mode=resident),
        ],
        out_specs=pl.BlockSpec((1, tm, D_MODEL), tile),
        out_shape=jax.ShapeDtypeStruct(h.shape, F32),
        scratch_shapes=[
            pltpu.VMEM((tm, D_INNER), BF16),
            pltpu.VMEM((tm, D_INNER), BF16),
        ],
        compiler_params=pltpu.CompilerParams(
            dimension_semantics=("arbitrary", "arbitrary"),
            vmem_limit_bytes=VMEM_LIMIT_BYTES),
        name="sgu_layer",
    )(h, norm_w.reshape(1, D_MODEL), w_in_all, ln_w.reshape(1, D_INNER),
      ln_b.reshape(1, D_INNER), w_s, b_s.T, w_out_all)


def _ssd_kernel(h_ref, hnext_ref, nw_ref, wzx_ref, wdt_ref, wdtt_ref, cw_ref, cbias_ref, dtb_ref, dtbt_ref,
                alog_ref, alogt_ref, dskip_ref, gnw_ref, wout_ref, fnw_ref,
                o_ref, raw_scr, tail_scr, hn_scr, dtr_scr, dttr_scr, st_scr, y_scr, yprev_scr, xprev_scr,
                *, tm, tiles_per_seq, final_norm):
    step = pl.program_id(0)

    slabs_per_group = GROUP_RAW // LANES

    def project_group(hn_new, g):
        c0 = D_INNER + g * GROUP_RAW
        raw = _dot(hn_new, wzx_ref[:, c0:c0 + GROUP_RAW])
        for i in range(slabs_per_group):
            raw_scr[g * slabs_per_group + i, CARRY_ROWS:CARRY_ROWS + tm, :] = raw[:, i * LANES:(i + 1) * LANES]

    def project_next(x_tile, groups):
        hn_new = _rms_rows(x_tile, nw_ref[...]).astype(BF16)
        for g in groups:
            project_group(hn_new, g)
        return hn_new, _dot(hn_new, wdt_ref[...]), _dot_nt(wdtt_ref[...], hn_new)

    @pl.when(step == 0)
    def _():
        hn_scr[...], dtr_scr[...], dttr_scr[...] = project_next(h_ref[0], range(N_GROUPS))
        yprev_scr[...] = jnp.zeros_like(yprev_scr)
        xprev_scr[...] = jnp.zeros_like(xprev_scr)

    @pl.when(step % tiles_per_seq == 0)
    def _():
        st_scr[...] = jnp.zeros_like(st_scr)
        raw_scr[:, 0:CARRY_ROWS, :] = jnp.zeros((CONV_DIM // LANES, CARRY_ROWS, LANES), F32)

    out_cols = D_MODEL // (N_GROUPS // 2)

    def finish_prev(part):
        cols = slice(part * out_cols, (part + 1) * out_cols)
        o_ref[0, :, cols] = _dot(yprev_scr[...], wout_ref[:, cols]) + xprev_scr[:, cols]

    x_in = h_ref[0]
    hn = hn_scr[...]
    dt = jax.nn.softplus(dtr_scr[...] + dtb_ref[...])
    dtt = jax.nn.softplus(dttr_scr[...] + dtbt_ref[...])
    hn_next, dtr_next, dttr_next = project_next(hnext_ref[0], ())
    da = dt * (-jnp.exp(alog_ref[...]))
    dat = dtt * (-jnp.exp(alogt_ref[...]))

    row = lax.broadcasted_iota(jnp.int32, (CHUNK, CHUNK), 0)
    col = lax.broadcasted_iota(jnp.int32, (CHUNK, CHUNK), 1)
    causal = row >= col
    first_head = col < HEADDIM
    lower = causal.astype(BF16)
    upper = (row <= col).astype(BF16)

    n_chunks = tm // CHUNK
    cs, cst, dte = [], [], []
    for c in range(n_chunks):
        r0, r1 = c * CHUNK, (c + 1) * CHUNK
        cs_c = sum(_dot(lower, p) for p in _split3(da[r0:r1]))
        cst_c = sum(_dot(p, upper) for p in _split3(dat[:, r0:r1]))
        last = cs_c[CHUNK - 1:CHUNK, :]
        cs.append(cs_c * LOG2E)
        cst.append((cst_c - jnp.log(dtt[:, r0:r1])) * LOG2E)
        dte.append((last - (cs_c - jnp.log(dt[r0:r1]))) * LOG2E)

    for g in range(N_GROUPS):
        lo, hi = g * GROUP_DIM, (g + 1) * GROUP_DIM
        base = g * GROUP_RAW

        def conv(c0, c1):
            slabs = []
            for n in range(c0 // LANES, c1 // LANES):
                acc = cbias_ref[:, n * LANES:(n + 1) * LANES]
                for k in range(D_CONV):
                    start = CARRY_ROWS - (D_CONV - 1) + k
                    tap = raw_scr[n, pl.ds(start, tm, stride=1), :]
                    acc = acc + cw_ref[k:k + 1, n * LANES:(n + 1) * LANES] * tap
                slabs.append(_silu(acc))
            return jnp.concatenate(slabs, axis=1)

        xg = conv(base, base + GROUP_DIM)
        bg = conv(base + GROUP_DIM, base + GROUP_DIM + D_STATE).astype(BF16)
        cg = conv(base + GROUP_DIM + D_STATE, base + GROUP_RAW).astype(BF16)
        gs = slice(g * slabs_per_group, (g + 1) * slabs_per_group)
        tail_scr[gs] = raw_scr[gs, tm:tm + CARRY_ROWS, :]
        project_group(hn_next, g)
        raw_scr[gs, 0:CARRY_ROWS, :] = tail_scr[gs]
        if g % 2 == 1:
            finish_prev(g // 2)
        zg = _dot(hn, wzx_ref[:, lo:hi])
        xg_bf = xg.astype(BF16)

        for c in range(n_chunks):
            r0, r1 = c * CHUNK, (c + 1) * CHUNK
            b_c, c_c, x_c = bg[r0:r1], cg[r0:r1], xg_bf[r0:r1]
            cb = _dot_nt(c_c, b_c)
            y_heads, ecs_b, dte_b = [], [], []
            for j in range(HEADS_PER_GROUP):
                hd = g * HEADS_PER_GROUP + j
                cs_l = jnp.broadcast_to(cs[c][:, hd:hd + 1], (CHUNK, CHUNK))
                seg = cs_l - cst[c][hd:hd + 1, :]
                decay_dt = jnp.exp2(jnp.where(causal, seg, -jnp.inf))
                w_ls = (cb * decay_dt).astype(BF16)
                y_heads.append(_dot(w_ls, x_c[:, j * HEADDIM:(j + 1) * HEADDIM]))
                ecs_b.append(jnp.exp2(cs_l))
                dte_b.append(jnp.exp2(jnp.broadcast_to(dte[c][:, hd:hd + 1], (CHUNK, CHUNK))))
            y_diag = jnp.concatenate(y_heads, axis=1)

            ecs_x = jnp.concatenate([jnp.where(first_head, ecs_b[j], ecs_b[j + 1])
                                     for j in range(0, HEADS_PER_GROUP, 2)], axis=1)
            dte_x = jnp.concatenate([jnp.where(first_head, dte_b[j], dte_b[j + 1])
                                     for j in range(0, HEADS_PER_GROUP, 2)], axis=1)
            cd_x = ecs_x[CHUNK - 1:CHUNK, :]

            state = st_scr[g]
            y_off = _dot(c_c, state.astype(BF16)) * ecs_x
            xw = (xg[r0:r1] * dte_x).astype(BF16)
            st_scr[g] = state * cd_x + _dot_tn(b_c, xw)

            y = y_diag + y_off + xg[r0:r1] * dskip_ref[:, lo:hi]
            gated = y * _silu(zg[r0:r1])
            y_scr[r0:r1, lo:hi] = _rms_rows(gated, gnw_ref[:, lo:hi]).astype(BF16)

    hn_scr[...], dtr_scr[...], dttr_scr[...] = hn_next, dtr_next, dttr_next
    if final_norm:
        o_ref[0] = _rms_rows(o_ref[0], fnw_ref[...])
    yprev_scr[...] = y_scr[...]
    xprev_scr[...] = x_in


def _xbc_by_group(a):
    lead = a.shape[:-1]
    return jnp.concatenate(
        [a[..., :D_INNER].reshape(lead + (N_GROUPS, GROUP_DIM)),
         a[..., D_INNER:D_INNER + BC_DIM].reshape(lead + (N_GROUPS, D_STATE)),
         a[..., D_INNER + BC_DIM:].reshape(lead + (N_GROUPS, D_STATE))], axis=-1).reshape(lead + (CONV_DIM,))


def _ssd_layer(h, norm_w, w_in_all, k, conv_w, conv_b, dt_bias, a_log, d_skip, gn_w, w_out_all,
               final_norm_w):
    bsz, seq, _ = h.shape
    tm = SSD_TOKEN_TILE
    tiles_per_seq = seq // tm
    n_tiles = bsz * tiles_per_seq
    const2 = lambda s: (0, 0)

    def tile(s):
        t = jnp.minimum(s, n_tiles - 1)
        return (t // tiles_per_seq, t % tiles_per_seq, 0)

    def prev_tile(s):
        t = jnp.maximum(s - 1, 0)
        return (t // tiles_per_seq, t % tiles_per_seq, 0)

    def next_tile(s):
        t = jnp.minimum(s + 1, n_tiles - 1)
        return (t // tiles_per_seq, t % tiles_per_seq, 0)

    resident = pl.Buffered(1)
    layer_k = lambda s: (k, 0, 0)
    w_dt = w_in_all[k, :, D_INNER + CONV_DIM:]
    final = final_norm_w is not None
    fnw = final_norm_w if final else jnp.ones((D_MODEL,), F32)
    small = lambda shape: pl.BlockSpec(shape, const2)
    return pl.pallas_call(
        functools.partial(_ssd_kernel, tm=tm, tiles_per_seq=tiles_per_seq, final_norm=final),
        grid=(n_tiles + 1,),
        in_specs=[
            pl.BlockSpec((1, tm, D_MODEL), tile),
            pl.BlockSpec((1, tm, D_MODEL), next_tile),
            small((1, D_MODEL)),
            pl.BlockSpec((None, D_MODEL, D_INNER + CONV_DIM), layer_k, pipeline_mode=resident),
            small((D_MODEL, N_HEADS)),
            small((N_HEADS, D_MODEL)),
            small((D_CONV, CONV_DIM)),
            small((1, CONV_DIM)),
            small((1, N_HEADS)),
            small((N_HEADS, 1)),
            small((1, N_HEADS)),
            small((N_HEADS, 1)),
            small((1, D_INNER)),
            small((1, D_INNER)),
            pl.BlockSpec((None, D_INNER, D_MODEL), layer_k, pipeline_mode=resident),
            small((1, D_MODEL)),
        ],
        out_specs=pl.BlockSpec((1, tm, D_MODEL), prev_tile),
        out_shape=jax.ShapeDtypeStruct(h.shape, F32),
        scratch_shapes=[
            pltpu.VMEM((CONV_DIM // LANES, CARRY_ROWS + tm, LANES), F32),
            pltpu.VMEM((CONV_DIM // LANES, CARRY_ROWS, LANES), F32),
            pltpu.VMEM((tm, D_MODEL), BF16),
            pltpu.VMEM((tm, N_HEADS), F32),
            pltpu.VMEM((N_HEADS, tm), F32),
            pltpu.VMEM((N_GROUPS, D_STATE, GROUP_DIM), F32),
            pltpu.VMEM((tm, D_INNER), BF16),
            pltpu.VMEM((tm, D_INNER), BF16),
            pltpu.VMEM((tm, D_MODEL), F32),
        ],
        compiler_params=pltpu.CompilerParams(
            dimension_semantics=("arbitrary",),
            vmem_limit_bytes=VMEM_LIMIT_BYTES),
        name="ssd_final" if final else "ssd_layer",
    )(h, h, norm_w.reshape(1, D_MODEL), w_in_all, w_dt, w_dt.T,
      _xbc_by_group(conv_w), _xbc_by_group(conv_b.reshape(1, CONV_DIM)),
      dt_bias.reshape(1, N_HEADS), dt_bias.reshape(N_HEADS, 1),
      a_log.reshape(1, N_HEADS), a_log.reshape(N_HEADS, 1),
      jnp.repeat(d_skip, HEADDIM).reshape(1, D_INNER), gn_w.reshape(1, D_INNER),
      w_out_all, fnw.reshape(1, D_MODEL))


def kernel(x, norm_w, final_norm_w, a_w_in, a_ln_w, a_ln_b, a_w_s, a_b_s, a_w_out,
           b_w_in, b_conv_w, b_conv_b, b_dt_bias, b_a_log, b_d_skip, b_norm_w, b_w_out):
    depth = norm_w.shape[0]
    a_w_in, a_w_out, b_w_in, b_w_out = (w.astype(BF16) for w in (a_w_in, a_w_out, b_w_in, b_w_out))
    b_w_in = jnp.concatenate([b_w_in[..., :D_INNER], _xbc_by_group(b_w_in[..., D_INNER:D_INNER + CONV_DIM]),
                              b_w_in[..., D_INNER + CONV_DIM:]], axis=-1)
    h = x
    for i in range(depth):
        k = i // 2
        if i % 2 == 0:
            h = _sgu_layer(h, norm_w[i], a_w_in, k, a_ln_w[k], a_ln_b[k], a_w_s[k], a_b_s[k], a_w_out)
        else:
            h = _ssd_layer(h, norm_w[i], b_w_in, k, b_conv_w[k], b_conv_b[k], b_dt_bias[k],
                           b_a_log[k], b_d_skip[k], b_norm_w[k], b_w_out,
                           final_norm_w if i == depth - 1 else None)
    return h
```

```python
import functools

import jax
import jax.numpy as jnp
from jax import lax
from jax.experimental import pallas as pl
from jax.experimental.pallas import tpu as pltpu

D_MODEL = 1024
D_INNER = 2048
CHUNK = 128
N_GROUPS = 8
GROUP_DIM = D_INNER // N_GROUPS
HEADDIM = 64
HEADS_PER_GROUP = GROUP_DIM // HEADDIM
N_HEADS = D_INNER // HEADDIM
D_STATE = 128
D_CONV = 4
BC_DIM = N_GROUPS * D_STATE
CONV_DIM = D_INNER + 2 * BC_DIM
NORM_EPS = 1e-6
LN_EPS = 1e-5
CARRY_ROWS = 8

SGU_TOKEN_TILE = 512
SSD_TOKEN_TILE = 256
LANES = 128
LOG2E = 1.4426950408889634
VMEM_LIMIT_BYTES = 56 * 1024 * 1024

BF16 = jnp.bfloat16
F32 = jnp.float32


def _dot(a, b):
    return jnp.dot(a, b, preferred_element_type=F32)


def _dot_nt(a, b):
    return lax.dot_general(a, b, (((1,), (1,)), ((), ())), preferred_element_type=F32)


def _dot_tn(a, b):
    return lax.dot_general(a, b, (((0,), (0,)), ((), ())), preferred_element_type=F32)


def _silu(x):
    half = 0.5 * x
    return half + half * jnp.tanh(half)


def _rms_rows(x, w):
    return x * lax.rsqrt(jnp.mean(x * x, axis=-1, keepdims=True) + NORM_EPS) * w


def _split3(a):
    hi = a.astype(BF16)
    r1 = a - hi.astype(F32)
    mid = r1.astype(BF16)
    lo = (r1 - mid.astype(F32)).astype(BF16)
    return hi, mid, lo


def _sgu_kernel(h_ref, nw_ref, win_ref, lnw_ref, lnb_ref, ws_ref, bst_ref, wout_ref,
                o_ref, v_scr, g_scr, *, tm):
    x = h_ref[0]
    hn = _rms_rows(x, nw_ref[...]).astype(BF16)

    v = _dot(hn, win_ref[:, 2 * D_INNER:3 * D_INNER])
    mu = jnp.mean(v, axis=-1, keepdims=True)
    vc = v - mu
    var = jnp.mean(vc * vc, axis=-1, keepdims=True)
    vn = vc * lax.rsqrt(var + LN_EPS) * lnw_ref[...] + lnb_ref[...]
    v_scr[...] = vn.astype(BF16)

    row = lax.broadcasted_iota(jnp.int32, (CHUNK, CHUNK), 0)
    col = lax.broadcasted_iota(jnp.int32, (CHUNK, CHUNK), 1)
    causal = row >= col
    for g in range(N_GROUPS):
        lo, hi = g * GROUP_DIM, (g + 1) * GROUP_DIM
        z = _dot(hn, win_ref[:, lo:hi])
        u = _dot(hn, win_ref[:, D_INNER + lo:D_INNER + hi])
        gate = u * _silu(z)
        wc = jnp.where(causal, ws_ref[g], 0.0).astype(BF16)
        bias = bst_ref[:, g:g + 1]
        for c in range(tm // CHUNK):
            r0, r1 = c * CHUNK, (c + 1) * CHUNK
            mixed = _dot(wc, v_scr[r0:r1, lo:hi]) + bias
            g_scr[r0:r1, lo:hi] = (gate[r0:r1] * mixed).astype(BF16)

    o_ref[0] = _dot(g_scr[...], wout_ref[...]) + x


def _sgu_layer(h, norm_w, w_in_all, k, ln_w, ln_b, w_s, b_s, w_out_all):
    bsz, seq, _ = h.shape
    tm = SGU_TOKEN_TILE
    const2 = lambda b, i: (0, 0)
    const3 = lambda b, i: (0, 0, 0)
    layer_k = lambda b, i: (k, 0, 0)
    tile = lambda b, i: (b, i, 0)
    resident = pl.Buffered(1)
    return pl.pallas_call(
        functools.partial(_sgu_kernel, tm=tm),
        grid=(bsz, seq // tm),
        in_specs=[
            pl.BlockSpec((1, tm, D_MODEL), tile),
            pl.BlockSpec((1, D_MODEL), const2),
            pl.BlockSpec((None, D_MODEL, 3 * D_INNER), layer_k, pipeline_mode=resident),
            pl.BlockSpec((1, D_INNER), const2),
            pl.BlockSpec((1, D_INNER), const2),
            pl.BlockSpec((N_GROUPS, CHUNK, CHUNK), const3),
            pl.BlockSpec((CHUNK, N_GROUPS), const2),
            pl.BlockSpec((None, D_INNER, D_MODEL), layer_k, pipeline_mode=resident),
        ],
        out_specs=pl.BlockSpec((1, tm, D_MODEL), tile),
        out_shape=jax.ShapeDtypeStruct(h.shape, F32),
        scratch_shapes=[
            pltpu.VMEM((tm, D_INNER), BF16),
            pltpu.VMEM((tm, D_INNER), BF16),
        ],
        compiler_params=pltpu.CompilerParams(
            dimension_semantics=("arbitrary", "arbitrary"),
            vmem_limit_bytes=VMEM_LIMIT_BYTES),
        name="sgu_layer",
    )(h, norm_w.reshape(1, D_MODEL), w_in_all, ln_w.reshape(1, D_INNER),
      ln_b.reshape(1, D_INNER), w_s, b_s.T, w_out_all)


def _ssd_kernel(h_ref, hnext_ref, nw_ref, wzx_ref, wdt_ref, wdtt_ref, cw_ref, cbias_ref, dtb_ref, dtbt_ref,
                alog_ref, alogt_ref, dskip_ref, gnw_ref, wout_ref, fnw_ref,
                o_ref, raw_scr, tail_scr, hn_scr, dtr_scr, dttr_scr, st_scr, y_scr, yprev_scr, xprev_scr,
                *, tm, tiles_per_seq, final_norm):
    step = pl.program_id(0)

    def project_next(x_tile):
        hn_new = _rms_rows(x_tile, nw_ref[...]).astype(BF16)
        hn_scr[...] = hn_new
        raw = _dot(hn_new, wzx_ref[:, D_INNER:])
        for n in range(CONV_DIM // LANES):
            raw_scr[n, CARRY_ROWS:CARRY_ROWS + tm, :] = raw[:, n * LANES:(n + 1) * LANES]
        dtr_scr[...] = _dot(hn_new, wdt_ref[...])
        dttr_scr[...] = _dot_nt(wdtt_ref[...], hn_new)

    @pl.when(step == 0)
    def _():
        project_next(h_ref[0])
        yprev_scr[...] = jnp.zeros_like(yprev_scr)
        xprev_scr[...] = jnp.zeros_like(xprev_scr)

    @pl.when(step % tiles_per_seq == 0)
    def _():
        st_scr[...] = jnp.zeros_like(st_scr)
        raw_scr[:, 0:CARRY_ROWS, :] = jnp.zeros((CONV_DIM // LANES, CARRY_ROWS, LANES), F32)

    o_ref[0] = _dot(yprev_scr[...], wout_ref[...]) + xprev_scr[...]

    x_in = h_ref[0]
    hn = hn_scr[...]
    dt = jax.nn.softplus(dtr_scr[...] + dtb_ref[...])
    dtt = jax.nn.softplus(dttr_scr[...] + dtbt_ref[...])
    da = dt * (-jnp.exp(alog_ref[...]))
    dat = dtt * (-jnp.exp(alogt_ref[...]))

    row = lax.broadcasted_iota(jnp.int32, (CHUNK, CHUNK), 0)
    col = lax.broadcasted_iota(jnp.int32, (CHUNK, CHUNK), 1)
    causal = row >= col
    first_head = col < HEADDIM
    lower = causal.astype(BF16)
    upper = (row <= col).astype(BF16)

    n_chunks = tm // CHUNK
    cs, cst, dte = [], [], []
    for c in range(n_chunks):
        r0, r1 = c * CHUNK, (c + 1) * CHUNK
        cs_c = sum(_dot(lower, p) for p in _split3(da[r0:r1]))
        cst_c = sum(_dot(p, upper) for p in _split3(dat[:, r0:r1]))
        last = cs_c[CHUNK - 1:CHUNK, :]
        cs.append(cs_c * LOG2E)
        cst.append((cst_c - jnp.log(dtt[:, r0:r1])) * LOG2E)
        dte.append((last - (cs_c - jnp.log(dt[r0:r1]))) * LOG2E)

    for g in range(N_GROUPS):
        lo, hi = g * GROUP_DIM, (g + 1) * GROUP_DIM
        blo, bhi = D_INNER + g * D_STATE, D_INNER + (g + 1) * D_STATE
        clo, chi = blo + BC_DIM, bhi + BC_DIM

        def conv(c0, c1):
            slabs = []
            for n in range(c0 // LANES, c1 // LANES):
                acc = cbias_ref[:, n * LANES:(n + 1) * LANES]
                for k in range(D_CONV):
                    start = CARRY_ROWS - (D_CONV - 1) + k
                    tap = raw_scr[n, pl.ds(start, tm, stride=1), :]
                    acc = acc + cw_ref[k:k + 1, n * LANES:(n + 1) * LANES] * tap
                slabs.append(_silu(acc))
            return jnp.concatenate(slabs, axis=1)

        xg = conv(lo, hi)
        bg = conv(blo, bhi).astype(BF16)
        cg = conv(clo, chi).astype(BF16)
        zg = _dot(hn, wzx_ref[:, lo:hi])
        xg_bf = xg.astype(BF16)

        for c in range(n_chunks):
            r0, r1 = c * CHUNK, (c + 1) * CHUNK
            b_c, c_c, x_c = bg[r0:r1], cg[r0:r1], xg_bf[r0:r1]
            cb = _dot_nt(c_c, b_c)
            y_heads, ecs_b, dte_b = [], [], []
            for j in range(HEADS_PER_GROUP):
                hd = g * HEADS_PER_GROUP + j
                cs_l = jnp.broadcast_to(cs[c][:, hd:hd + 1], (CHUNK, CHUNK))
                seg = cs_l - cst[c][hd:hd + 1, :]
                decay_dt = jnp.exp2(jnp.where(causal, seg, -jnp.inf))
                w_ls = (cb * decay_dt).astype(BF16)
                y_heads.append(_dot(w_ls, x_c[:, j * HEADDIM:(j + 1) * HEADDIM]))
                ecs_b.append(jnp.exp2(cs_l))
                dte_b.append(jnp.exp2(jnp.broadcast_to(dte[c][:, hd:hd + 1], (CHUNK, CHUNK))))
            y_diag = jnp.concatenate(y_heads, axis=1)

            ecs_x = jnp.concatenate([jnp.where(first_head, ecs_b[j], ecs_b[j + 1])
                                     for j in range(0, HEADS_PER_GROUP, 2)], axis=1)
            dte_x = jnp.concatenate([jnp.where(first_head, dte_b[j], dte_b[j + 1])
                                     for j in range(0, HEADS_PER_GROUP, 2)], axis=1)
            cd_x = ecs_x[CHUNK - 1:CHUNK, :]

            state = st_scr[g]
            y_off = _dot(c_c, state.astype(BF16)) * ecs_x
            xw = (xg[r0:r1] * dte_x).astype(BF16)
            st_scr[g] = state * cd_x + _dot_tn(b_c, xw)

            y = y_diag + y_off + xg[r0:r1] * dskip_ref[:, lo:hi]
            gated = y * _silu(zg[r0:r1])
            y_scr[r0:r1, lo:hi] = _rms_rows(gated, gnw_ref[:, lo:hi]).astype(BF16)

    tail_scr[...] = raw_scr[:, tm:tm + CARRY_ROWS, :]
    project_next(hnext_ref[0])
    raw_scr[:, 0:CARRY_ROWS, :] = tail_scr[...]

    if final_norm:
        o_ref[0] = _rms_rows(o_ref[0], fnw_ref[...])
    yprev_scr[...] = y_scr[...]
    xprev_scr[...] = x_in


def _ssd_layer(h, norm_w, w_in_all, k, conv_w, conv_b, dt_bias, a_log, d_skip, gn_w, w_out_all,
               final_norm_w):
    bsz, seq, _ = h.shape
    tm = SSD_TOKEN_TILE
    tiles_per_seq = seq // tm
    n_tiles = bsz * tiles_per_seq
    const2 = lambda s: (0, 0)

    def tile(s):
        t = jnp.minimum(s, n_tiles - 1)
        return (t // tiles_per_seq, t % tiles_per_seq, 0)

    def prev_tile(s):
        t = jnp.maximum(s - 1, 0)
        return (t // tiles_per_seq, t % tiles_per_seq, 0)

    def next_tile(s):
        t = jnp.minimum(s + 1, n_tiles - 1)
        return (t // tiles_per_seq, t % tiles_per_seq, 0)

    resident = pl.Buffered(1)
    layer_k = lambda s: (k, 0, 0)
    w_dt = w_in_all[k, :, D_INNER + CONV_DIM:]
    final = final_norm_w is not None
    fnw = final_norm_w if final else jnp.ones((D_MODEL,), F32)
    small = lambda shape: pl.BlockSpec(shape, const2)
    return pl.pallas_call(
        functools.partial(_ssd_kernel, tm=tm, tiles_per_seq=tiles_per_seq, final_norm=final),
        grid=(n_tiles + 1,),
        in_specs=[
            pl.BlockSpec((1, tm, D_MODEL), tile),
            pl.BlockSpec((1, tm, D_MODEL), next_tile),
            small((1, D_MODEL)),
            pl.BlockSpec((None, D_MODEL, D_INNER + CONV_DIM), layer_k, pipeline_mode=resident),
            small((D_MODEL, N_HEADS)),
            small((N_HEADS, D_MODEL)),
            small((D_CONV, CONV_DIM)),
            small((1, CONV_DIM)),
            small((1, N_HEADS)),
            small((N_HEADS, 1)),
            small((1, N_HEADS)),
            small((N_HEADS, 1)),
            small((1, D_INNER)),
            small((1, D_INNER)),
            pl.BlockSpec((None, D_INNER, D_MODEL), layer_k, pipeline_mode=resident),
            small((1, D_MODEL)),
        ],
        out_specs=pl.BlockSpec((1, tm, D_MODEL), prev_tile),
        out_shape=jax.ShapeDtypeStruct(h.shape, F32),
        scratch_shapes=[
            pltpu.VMEM((CONV_DIM // LANES, CARRY_ROWS + tm, LANES), F32),
            pltpu.VMEM((CONV_DIM // LANES, CARRY_ROWS, LANES), F32),
            pltpu.VMEM((tm, D_MODEL), BF16),
            pltpu.VMEM((tm, N_HEADS), F32),
            pltpu.VMEM((N_HEADS, tm), F32),
            pltpu.VMEM((N_GROUPS, D_STATE, GROUP_DIM), F32),
            pltpu.VMEM((tm, D_INNER), BF16),
            pltpu.VMEM((tm, D_INNER), BF16),
            pltpu.VMEM((tm, D_MODEL), F32),
        ],
        compiler_params=pltpu.CompilerParams(
            dimension_semantics=("arbitrary",),
            vmem_limit_bytes=VMEM_LIMIT_BYTES),
        name="ssd_final" if final else "ssd_layer",
    )(h, h, norm_w.reshape(1, D_MODEL), w_in_all, w_dt, w_dt.T, conv_w, conv_b.reshape(1, CONV_DIM),
      dt_bias.reshape(1, N_HEADS), dt_bias.reshape(N_HEADS, 1),
      a_log.reshape(1, N_HEADS), a_log.reshape(N_HEADS, 1),
      jnp.repeat(d_skip, HEADDIM).reshape(1, D_INNER), gn_w.reshape(1, D_INNER),
      w_out_all, fnw.reshape(1, D_MODEL))


def kernel(x, norm_w, final_norm_w, a_w_in, a_ln_w, a_ln_b, a_w_s, a_b_s, a_w_out,
           b_w_in, b_conv_w, b_conv_b, b_dt_bias, b_a_log, b_d_skip, b_norm_w, b_w_out):
    depth = norm_w.shape[0]
    a_w_in, a_w_out, b_w_in, b_w_out = (w.astype(BF16) for w in (a_w_in, a_w_out, b_w_in, b_w_out))
    h = x
    for i in range(depth):
        k = i // 2
        if i % 2 == 0:
            h = _sgu_layer(h, norm_w[i], a_w_in, k, a_ln_w[k], a_ln_b[k], a_w_s[k], a_b_s[k], a_w_out)
        else:
            h = _ssd_layer(h, norm_w[i], b_w_in, k, b_conv_w[k], b_conv_b[k], b_dt_bias[k],
                           b_a_log[k], b_d_skip[k], b_norm_w[k], b_w_out,
                           final_norm_w if i == depth - 1 else None)
    return h
```

```python
import functools

import jax
import jax.numpy as jnp
from jax import lax
from jax.experimental import pallas as pl
from jax.experimental.pallas import tpu as pltpu

D_MODEL = 1024
D_INNER = 2048
CHUNK = 128
N_GROUPS = 8
GROUP_DIM = D_INNER // N_GROUPS
HEADDIM = 64
HEADS_PER_GROUP = GROUP_DIM // HEADDIM
N_HEADS = D_INNER // HEADDIM
D_STATE = 128
D_CONV = 4
BC_DIM = N_GROUPS * D_STATE
CONV_DIM = D_INNER + 2 * BC_DIM
NORM_EPS = 1e-6
LN_EPS = 1e-5
CARRY_ROWS = 8

SGU_TOKEN_TILE = 512
SSD_TOKEN_TILE = 256
LANES = 128
LOG2E = 1.4426950408889634
VMEM_LIMIT_BYTES = 56 * 1024 * 1024

BF16 = jnp.bfloat16
F32 = jnp.float32


def _dot(a, b):
    return jnp.dot(a, b, preferred_element_type=F32)


def _dot_nt(a, b):
    return lax.dot_general(a, b, (((1,), (1,)), ((), ())), preferred_element_type=F32)


def _dot_tn(a, b):
    return lax.dot_general(a, b, (((0,), (0,)), ((), ())), preferred_element_type=F32)


def _silu(x):
    half = 0.5 * x
    return half + half * jnp.tanh(half)


def _rms_rows(x, w):
    return x * lax.rsqrt(jnp.mean(x * x, axis=-1, keepdims=True) + NORM_EPS) * w


def _split3(a):
    hi = a.astype(BF16)
    r1 = a - hi.astype(F32)
    mid = r1.astype(BF16)
    lo = (r1 - mid.astype(F32)).astype(BF16)
    return hi, mid, lo


def _sgu_kernel(h_ref, nw_ref, win_ref, lnw_ref, lnb_ref, ws_ref, bst_ref, wout_ref,
                o_ref, v_scr, g_scr, *, tm):
    x = h_ref[0]
    hn = _rms_rows(x, nw_ref[...]).astype(BF16)

    v = _dot(hn, win_ref[:, 2 * D_INNER:3 * D_INNER])
    mu = jnp.mean(v, axis=-1, keepdims=True)
    vc = v - mu
    var = jnp.mean(vc * vc, axis=-1, keepdims=True)
    vn = vc * lax.rsqrt(var + LN_EPS) * lnw_ref[...] + lnb_ref[...]
    v_scr[...] = vn.astype(BF16)

    row = lax.broadcasted_iota(jnp.int32, (CHUNK, CHUNK), 0)
    col = lax.broadcasted_iota(jnp.int32, (CHUNK, CHUNK), 1)
    causal = row >= col
    for g in range(N_GROUPS):
        lo, hi = g * GROUP_DIM, (g + 1) * GROUP_DIM
        z = _dot(hn, win_ref[:, lo:hi])
        u = _dot(hn, win_ref[:, D_INNER + lo:D_INNER + hi])
        gate = u * _silu(z)
        wc = jnp.where(causal, ws_ref[g], 0.0).astype(BF16)
        bias = bst_ref[:, g:g + 1]
        for c in range(tm // CHUNK):
            r0, r1 = c * CHUNK, (c + 1) * CHUNK
            mixed = _dot(wc, v_scr[r0:r1, lo:hi]) + bias
            g_scr[r0:r1, lo:hi] = (gate[r0:r1] * mixed).astype(BF16)

    o_ref[0] = _dot(g_scr[...], wout_ref[...]) + x


def _sgu_layer(h, norm_w, w_in_all, k, ln_w, ln_b, w_s, b_s, w_out_all):
    bsz, seq, _ = h.shape
    tm = SGU_TOKEN_TILE
    const2 = lambda b, i: (0, 0)
    const3 = lambda b, i: (0, 0, 0)
    layer_k = lambda b, i: (k, 0, 0)
    tile = lambda b, i: (b, i, 0)
    resident = pl.Buffered(1)
    return pl.pallas_call(
        functools.partial(_sgu_kernel, tm=tm),
        grid=(bsz, seq // tm),
        in_specs=[
            pl.BlockSpec((1, tm, D_MODEL), tile),
            pl.BlockSpec((1, D_MODEL), const2),
            pl.BlockSpec((None, D_MODEL, 3 * D_INNER), layer_k, pipeline_mode=resident),
            pl.BlockSpec((1, D_INNER), const2),
            pl.BlockSpec((1, D_INNER), const2),
            pl.BlockSpec((N_GROUPS, CHUNK, CHUNK), const3),
            pl.BlockSpec((CHUNK, N_GROUPS), const2),
            pl.BlockSpec((None, D_INNER, D_MODEL), layer_k, pipeline_mode=resident),
        ],
        out_specs=pl.BlockSpec((1, tm, D_MODEL), tile),
        out_shape=jax.ShapeDtypeStruct(h.shape, F32),
        scratch_shapes=[
            pltpu.VMEM((tm, D_INNER), BF16),
            pltpu.VMEM((tm, D_INNER), BF16),
        ],
        compiler_params=pltpu.CompilerParams(
            dimension_semantics=("arbitrary", "arbitrary"),
            vmem_limit_bytes=VMEM_LIMIT_BYTES),
        name="sgu_layer",
    )(h, norm_w.reshape(1, D_MODEL), w_in_all, ln_w.reshape(1, D_INNER),
      ln_b.reshape(1, D_INNER), w_s, b_s.T, w_out_all)


def _ssd_kernel(h_ref, hnext_ref, nw_ref, wzx_ref, wdt_ref, wdtt_ref, cw_ref, cbias_ref, dtb_ref, dtbt_ref,
                alog_ref, alogt_ref, dskip_ref, gnw_ref, wout_ref, fnw_ref,
                o_ref, raw_scr, tail_scr, hn_scr, dtr_scr, dttr_scr, st_scr, y_scr, yprev_scr, xprev_scr,
                *, tm, tiles_per_seq, final_norm):
    step = pl.program_id(0)

    def project_next(x_tile):
        hn_new = _rms_rows(x_tile, nw_ref[...]).astype(BF16)
        hn_scr[...] = hn_new
        raw = _dot(hn_new, wzx_ref[:, D_INNER:])
        for n in range(CONV_DIM // LANES):
            raw_scr[n, CARRY_ROWS:CARRY_ROWS + tm, :] = raw[:, n * LANES:(n + 1) * LANES]
        dtr_scr[...] = _dot(hn_new, wdt_ref[...])
        dttr_scr[...] = _dot_nt(wdtt_ref[...], hn_new)

    @pl.when(step == 0)
    def _():
        project_next(h_ref[0])
        yprev_scr[...] = jnp.zeros_like(yprev_scr)
        xprev_scr[...] = jnp.zeros_like(xprev_scr)

    @pl.when(step % tiles_per_seq == 0)
    def _():
        st_scr[...] = jnp.zeros_like(st_scr)
        raw_scr[:, 0:CARRY_ROWS, :] = jnp.zeros((CONV_DIM // LANES, CARRY_ROWS, LANES), F32)

    o_ref[0] = _dot(yprev_scr[...], wout_ref[...]) + xprev_scr[...]

    x_in = h_ref[0]
    hn = hn_scr[...]
    dt = jax.nn.softplus(dtr_scr[...] + dtb_ref[...])
    dtt = jax.nn.softplus(dttr_scr[...] + dtbt_ref[...])
    da = dt * (-jnp.exp(alog_ref[...]))
    dat = dtt * (-jnp.exp(alogt_ref[...]))

    row = lax.broadcasted_iota(jnp.int32, (CHUNK, CHUNK), 0)
    col = lax.broadcasted_iota(jnp.int32, (CHUNK, CHUNK), 1)
    causal = row >= col
    first_head = col < HEADDIM
    lower = causal.astype(BF16)
    upper = (row <= col).astype(BF16)

    n_chunks = tm // CHUNK
    cs, cst, dte = [], [], []
    for c in range(n_chunks):
        r0, r1 = c * CHUNK, (c + 1) * CHUNK
        cs_c = sum(_dot(lower, p) for p in _split3(da[r0:r1]))
        cst_c = sum(_dot(p, upper) for p in _split3(dat[:, r0:r1]))
        last = cs_c[CHUNK - 1:CHUNK, :]
        cs.append(cs_c * LOG2E)
        cst.append((cst_c - jnp.log(dtt[:, r0:r1])) * LOG2E)
        dte.append((last - (cs_c - jnp.log(dt[r0:r1]))) * LOG2E)

    for g in range(N_GROUPS):
        lo, hi = g * GROUP_DIM, (g + 1) * GROUP_DIM
        blo, bhi = D_INNER + g * D_STATE, D_INNER + (g + 1) * D_STATE
        clo, chi = blo + BC_DIM, bhi + BC_DIM

        def conv(c0, c1):
            slabs = []
            for n in range(c0 // LANES, c1 // LANES):
                acc = cbias_ref[:, n * LANES:(n + 1) * LANES]
                for k in range(D_CONV):
                    start = CARRY_ROWS - (D_CONV - 1) + k
                    tap = raw_scr[n, pl.ds(start, tm, stride=1), :]
                    acc = acc + cw_ref[k:k + 1, n * LANES:(n + 1) * LANES] * tap
                slabs.append(_silu(acc))
            return jnp.concatenate(slabs, axis=1)

        xg = conv(lo, hi)
        bg = conv(blo, bhi).astype(BF16)
        cg = conv(clo, chi).astype(BF16)
        zg = _dot(hn, wzx_ref[:, lo:hi])
        xg_bf = xg.astype(BF16)

        for c in range(n_chunks):
            r0, r1 = c * CHUNK, (c + 1) * CHUNK
            b_c, c_c, x_c = bg[r0:r1], cg[r0:r1], xg_bf[r0:r1]
            cb = _dot_nt(c_c, b_c)
            y_heads, ecs_b, dte_b = [], [], []
            for j in range(HEADS_PER_GROUP):
                hd = g * HEADS_PER_GROUP + j
                cs_l = jnp.broadcast_to(cs[c][:, hd:hd + 1], (CHUNK, CHUNK))
                seg = cs_l - cst[c][hd:hd + 1, :]
                decay_dt = jnp.exp2(jnp.where(causal, seg, -jnp.inf))
                w_ls = (cb * decay_dt).astype(BF16)
                y_heads.append(_dot(w_ls, x_c[:, j * HEADDIM:(j + 1) * HEADDIM]))
                ecs_b.append(jnp.exp2(cs_l))
                dte_b.append(jnp.exp2(jnp.broadcast_to(dte[c][:, hd:hd + 1], (CHUNK, CHUNK))))
            y_diag = jnp.concatenate(y_heads, axis=1)

            ecs_x = jnp.concatenate([jnp.where(first_head, ecs_b[j], ecs_b[j + 1])
                                     for j in range(0, HEADS_PER_GROUP, 2)], axis=1)
            dte_x = jnp.concatenate([jnp.where(first_head, dte_b[j], dte_b[j + 1])
                                     for j in range(0, HEADS_PER_GROUP, 2)], axis=1)
            cd_x = ecs_x[CHUNK - 1:CHUNK, :]

            state = st_scr[g]
            y_off = _dot(c_c, state.astype(BF16)) * ecs_x
            xw = (xg[r0:r1] * dte_x).astype(BF16)
            st_scr[g] = state * cd_x + _dot_tn(b_c, xw)

            y = y_diag + y_off + xg[r0:r1] * dskip_ref[:, lo:hi]
            gated = y * _silu(zg[r0:r1])
            y_scr[r0:r1, lo:hi] = _rms_rows(gated, gnw_ref[:, lo:hi]).astype(BF16)

    tail_scr[...] = raw_scr[:, tm:tm + CARRY_ROWS, :]
    project_next(hnext_ref[0])
    raw_scr[:, 0:CARRY_ROWS, :] = tail_scr[...]

    if final_norm:
        o_ref[0] = _rms_rows(o_ref[0], fnw_ref[...])
    yprev_scr[...] = y_scr[...]
    xprev_scr[...] = x_in


def _ssd_layer(h, norm_w, w_in_all, k, conv_w, conv_b, dt_bias, a_log, d_skip, gn_w, w_out_all,
               final_norm_w):
    bsz, seq, _ = h.shape
    tm = SSD_TOKEN_TILE
    tiles_per_seq = seq // tm
    n_tiles = bsz * tiles_per_seq
    const2 = lambda s: (0, 0)

    def tile(s):
        t = jnp.minimum(s, n_tiles - 1)
        return (t // tiles_per_seq, t % tiles_per_seq, 0)

    def prev_tile(s):
        t = jnp.maximum(s - 1, 0)
        return (t // tiles_per_seq, t % tiles_per_seq, 0)

    def next_tile(s):
        t = jnp.minimum(s + 1, n_tiles - 1)
        return (t // tiles_per_seq, t % tiles_per_seq, 0)

    resident = pl.Buffered(1)
    layer_k = lambda s: (k, 0, 0)
    w_dt = w_in_all[k, :, D_INNER + CONV_DIM:D_INNER + CONV_DIM + N_HEADS]
    final = final_norm_w is not None
    fnw = final_norm_w if final else jnp.ones((D_MODEL,), F32)
    small = lambda shape: pl.BlockSpec(shape, const2)
    return pl.pallas_call(
        functools.partial(_ssd_kernel, tm=tm, tiles_per_seq=tiles_per_seq, final_norm=final),
        grid=(n_tiles + 1,),
        in_specs=[
            pl.BlockSpec((1, tm, D_MODEL), tile),
            pl.BlockSpec((1, tm, D_MODEL), next_tile),
            small((1, D_MODEL)),
            pl.BlockSpec((None, D_MODEL, D_INNER + CONV_DIM), layer_k, pipeline_mode=resident),
            small((D_MODEL, N_HEADS)),
            small((N_HEADS, D_MODEL)),
            small((D_CONV, CONV_DIM)),
            small((1, CONV_DIM)),
            small((1, N_HEADS)),
            small((N_HEADS, 1)),
            small((1, N_HEADS)),
            small((N_HEADS, 1)),
            small((1, D_INNER)),
            small((1, D_INNER)),
            pl.BlockSpec((None, D_INNER, D_MODEL), layer_k, pipeline_mode=resident),
            small((1, D_MODEL)),
        ],
        out_specs=pl.BlockSpec((1, tm, D_MODEL), prev_tile),
        out_shape=jax.ShapeDtypeStruct(h.shape, F32),
        scratch_shapes=[
            pltpu.VMEM((CONV_DIM // LANES, CARRY_ROWS + tm, LANES), F32),
            pltpu.VMEM((CONV_DIM // LANES, CARRY_ROWS, LANES), F32),
            pltpu.VMEM((tm, D_MODEL), BF16),
            pltpu.VMEM((tm, N_HEADS), F32),
            pltpu.VMEM((N_HEADS, tm), F32),
            pltpu.VMEM((N_GROUPS, D_STATE, GROUP_DIM), F32),
            pltpu.VMEM((tm, D_INNER), BF16),
            pltpu.VMEM((tm, D_INNER), BF16),
            pltpu.VMEM((tm, D_MODEL), F32),
        ],
        compiler_params=pltpu.CompilerParams(
            dimension_semantics=("arbitrary",),
            vmem_limit_bytes=VMEM_LIMIT_BYTES),
        name="ssd_final" if final else "ssd_layer",
    )(h, h, norm_w.reshape(1, D_MODEL), w_in_all, w_dt, w_dt.T, conv_w, conv_b.reshape(1, CONV_DIM),
      dt_bias.reshape(1, N_HEADS), dt_bias.reshape(N_HEADS, 1),
      a_log.reshape(1, N_HEADS), a_log.reshape(N_HEADS, 1),
      jnp.repeat(d_skip, HEADDIM).reshape(1, D_INNER), gn_w.reshape(1, D_INNER),
      w_out_all, fnw.reshape(1, D_MODEL))


def kernel(x, norm_w, final_norm_w, a_w_in, a_ln_w, a_ln_b, a_w_s, a_b_s, a_w_out,
           b_w_in, b_conv_w, b_conv_b, b_dt_bias, b_a_log, b_d_skip, b_norm_w, b_w_out):
    depth = norm_w.shape[0]
    a_w_in, a_w_out, b_w_out = (w.astype(BF16) for w in (a_w_in, a_w_out, b_w_out))
    lane_pad = -b_w_in.shape[-1] % LANES
    b_w_in = jnp.pad(b_w_in, ((0, 0), (0, 0), (0, lane_pad))).astype(BF16)
    h = x
    for i in range(depth):
        k = i // 2
        if i % 2 == 0:
            h = _sgu_layer(h, norm_w[i], a_w_in, k, a_ln_w[k], a_ln_b[k], a_w_s[k], a_b_s[k], a_w_out)
        else:
            h = _ssd_layer(h, norm_w[i], b_w_in, k, b_conv_w[k], b_conv_b[k], b_dt_bias[k],
                           b_a_log[k], b_d_skip[k], b_norm_w[k], b_w_out,
                           final_norm_w if i == depth - 1 else None)
    return h
```
